```python
import jax, jax.numpy as jnp
from jax import lax
import numpy as np

D_MODEL = 1024
BATCH = 16
SEQ = 4096
DEPTH = 1

HEAD_DIM = 64
N_HEADS = D_MODEL // HEAD_DIM
N_HEADS_A = N_HEADS // 2
N_KV_A = N_HEADS_A // 4
N_HEADS_B = N_HEADS - N_HEADS_A
N_KV_B = N_HEADS_B // 4
D_FF = 4 * D_MODEL
GRID_W = 64
BLOCK = 128
WINDOW = 128
N_BUCKETS = 32
MAX_DISTANCE = 128
ROPE_THETA = 10000.0
EPS = 1e-6
NEG_INF = -1e30
IN_WIDTHS = (N_HEADS_A * HEAD_DIM, N_KV_A * HEAD_DIM, N_KV_A * HEAD_DIM,
             N_HEADS_B * HEAD_DIM, N_KV_B * HEAD_DIM, N_KV_B * HEAD_DIM)
IN_TOTAL = sum(IN_WIDTHS)

kernel_name = "hybrid_axial_global_window_sink_encoder"


def rmsnorm(x, g):
    xf = x.astype(jnp.float32)
    y = xf * lax.rsqrt(jnp.mean(xf * xf, axis=-1, keepdims=True) + EPS)
    return (y * g.astype(jnp.float32)).astype(x.dtype)


def _rope_half(x, ang):
    cos = jnp.cos(ang).astype(x.dtype)
    sin = jnp.sin(ang).astype(x.dtype)
    x1, x2 = jnp.split(x, 2, axis=-1)
    return jnp.concatenate([x1 * cos - x2 * sin, x2 * cos + x1 * sin], axis=-1)


def axial_rope(x, row, col):
    half = HEAD_DIM // 2
    nf = half // 2
    freqs = ROPE_THETA ** (-jnp.arange(nf, dtype=jnp.float32) / nf)
    ang_r = row.astype(jnp.float32)[:, None] * freqs[None, :]
    ang_c = col.astype(jnp.float32)[:, None] * freqs[None, :]
    return jnp.concatenate([_rope_half(x[..., :half], ang_r),
                            _rope_half(x[..., half:], ang_c)], axis=-1)


def t5_bucket(rel):
    nb = N_BUCKETS // 2
    ret = (rel > 0).astype(jnp.int32) * nb
    n = jnp.abs(rel)
    max_exact = nb // 2
    nf = jnp.maximum(n, 1).astype(jnp.float32)
    large = max_exact + (jnp.log(nf / max_exact) / np.float32(np.log(MAX_DISTANCE / max_exact))
                         * (nb - max_exact)).astype(jnp.int32)
    large = jnp.minimum(large, nb - 1)
    return ret + jnp.where(n < max_exact, n, large)


def global_axial_gqa(q, k, v, gq, gk):
    bsz, s_len = q.shape[0], q.shape[1]
    rows = s_len // GRID_W
    row = jnp.repeat(jnp.arange(rows, dtype=jnp.int32), GRID_W)
    col = jnp.tile(jnp.arange(GRID_W, dtype=jnp.int32), rows)
    q = rmsnorm(q, gq).transpose(0, 2, 1, 3)
    k = rmsnorm(k, gk).transpose(0, 2, 1, 3)
    v = v.transpose(0, 2, 1, 3)
    q = axial_rope(q, row, col)
    k = axial_rope(k, row, col)
    grp = N_HEADS_A // N_KV_A
    nblk = s_len // BLOCK
    qb = jnp.moveaxis(q.reshape(bsz, N_KV_A, grp, nblk, BLOCK, HEAD_DIM), 3, 0)
    scale = HEAD_DIM ** -0.5

    def one_block(qblk):
        s = jnp.einsum('bkgqd,bksd->bkgqs', qblk, k).astype(jnp.float32) * scale
        p = jax.nn.softmax(s, axis=-1).astype(v.dtype)
        return jnp.einsum('bkgqs,bksd->bkgqd', p, v)

    o = lax.map(one_block, qb)
    return o.transpose(1, 0, 4, 2, 3, 5).reshape(bsz, s_len, N_HEADS_A * HEAD_DIM)


def window_sink_gqa(q, k, v, sink, rel_table):
    bsz, s_len = q.shape[0], q.shape[1]
    grp = N_HEADS_B // N_KV_B
    nblk = s_len // BLOCK
    span = 3 * BLOCK
    qb = jnp.moveaxis(q.transpose(0, 2, 1, 3).reshape(bsz, N_KV_B, grp, nblk, BLOCK, HEAD_DIM), 3, 0)
    pad = ((0, 0), (0, 0), (BLOCK, BLOCK), (0, 0))
    kp = jnp.pad(k.transpose(0, 2, 1, 3), pad)
    vp = jnp.pad(v.transpose(0, 2, 1, 3), pad)
    a = jnp.arange(BLOCK, dtype=jnp.int32)
    c = jnp.arange(span, dtype=jnp.int32)
    rel = c[None, :] - BLOCK - a[:, None]
    band = jnp.abs(rel) <= WINDOW
    bias = rel_table[t5_bucket(rel)]
    bias = bias.transpose(2, 0, 1).reshape(N_KV_B, grp, BLOCK, span).astype(jnp.float32)
    sink_l = sink.reshape(N_KV_B, grp, 1).astype(jnp.float32)
    scale = HEAD_DIM ** -0.5

    def one_block(args):
        qblk, n = args
        start = n * BLOCK
        kblk = lax.dynamic_slice_in_dim(kp, start, span, axis=2)
        vblk = lax.dynamic_slice_in_dim(vp, start, span, axis=2)
        kpos = start - BLOCK + c
        valid = band & ((kpos >= 0) & (kpos < s_len))[None, :]
        s = jnp.einsum('bkgqd,bksd->bkgqs', qblk, kblk).astype(jnp.float32) * scale + bias
        s = jnp.where(valid, s, NEG_INF)
        m = jnp.maximum(jnp.max(s, axis=-1), sink_l)
        p = jnp.exp(s - m[..., None])
        denom = jnp.sum(p, axis=-1) + jnp.exp(sink_l - m)
        w = (p / denom[..., None]).astype(vblk.dtype)
        return jnp.einsum('bkgqs,bksd->bkgqd', w, vblk)

    o = lax.map(one_block, (qb, jnp.arange(nblk, dtype=jnp.int32)))
    return o.transpose(1, 0, 4, 2, 3, 5).reshape(bsz, s_len, N_HEADS_B * HEAD_DIM)


def setup_inputs(seed: int = 0) -> dict:
    key = jax.random.key(seed)
    ks = jax.random.split(key, 16)
    f32 = jnp.float32

    def gain(k, shape):
        return 1.0 + 0.02 * jax.random.normal(k, shape, f32)

    return {
        "x": jax.random.normal(ks[0], (BATCH, SEQ, D_MODEL), f32),
        "w_in": jax.random.normal(ks[1], (DEPTH, D_MODEL, IN_TOTAL), f32) * D_MODEL ** -0.5,
        "w_o": jax.random.normal(ks[2], (DEPTH, D_MODEL, D_MODEL), f32) * D_MODEL ** -0.5,
        "g_pre_mix": gain(ks[3], (DEPTH, D_MODEL)),
        "g_post_mix": gain(ks[4], (DEPTH, D_MODEL)),
        "q_norm_a": gain(ks[5], (DEPTH, HEAD_DIM)),
        "k_norm_a": gain(ks[6], (DEPTH, HEAD_DIM)),
        "sink_b": 0.5 * jax.random.normal(ks[7], (DEPTH, N_HEADS_B), f32),
        "rel_bias": 0.5 * jax.random.normal(ks[8], (N_BUCKETS, N_HEADS_B), f32),
        "g_pre_ffn": gain(ks[9], (DEPTH, D_MODEL)),
        "w_ffn_up": jax.random.normal(ks[10], (DEPTH, D_MODEL, D_FF), f32) * D_MODEL ** -0.5,
        "w_ffn_down": jax.random.normal(ks[11], (DEPTH, D_FF, D_MODEL), f32) * D_FF ** -0.5,
        "g_post_ffn": gain(ks[12], (DEPTH, D_MODEL)),
    }


def reference(x, w_in, w_o, g_pre_mix, g_post_mix, q_norm_a, k_norm_a, sink_b, rel_bias,
              g_pre_ffn, w_ffn_up, w_ffn_down, g_post_ffn):
    bsz, s_len, _ = x.shape
    split_idx = [int(i) for i in np.cumsum(IN_WIDTHS)[:-1]]
    for l in range(DEPTH):
        h = rmsnorm(x, g_pre_mix[l])
        proj = h @ w_in[l]
        qa, ka, va, qb, kb, vb = jnp.split(proj, split_idx, axis=-1)
        qa = qa.reshape(bsz, s_len, N_HEADS_A, HEAD_DIM)
        ka = ka.reshape(bsz, s_len, N_KV_A, HEAD_DIM)
        va = va.reshape(bsz, s_len, N_KV_A, HEAD_DIM)
        qb = qb.reshape(bsz, s_len, N_HEADS_B, HEAD_DIM)
        kb = kb.reshape(bsz, s_len, N_KV_B, HEAD_DIM)
        vb = vb.reshape(bsz, s_len, N_KV_B, HEAD_DIM)
        out_a = global_axial_gqa(qa, ka, va, q_norm_a[l], k_norm_a[l])
        out_b = window_sink_gqa(qb, kb, vb, sink_b[l], rel_bias)
        mix = jnp.concatenate([out_a, out_b], axis=-1) @ w_o[l]
        x = x + rmsnorm(mix, g_post_mix[l])
        h = rmsnorm(x, g_pre_ffn[l])
        u = jax.nn.relu(h @ w_ffn_up[l])
        f = (u * u) @ w_ffn_down[l]
        x = x + rmsnorm(f, g_post_ffn[l])
    return x
```

```python
import functools
import math

import numpy as np
import jax
import jax.numpy as jnp
from jax import lax
from jax.experimental import pallas as pl
from jax.experimental.pallas import tpu as pltpu

D_MODEL = 1024
HEAD_DIM = 64
N_HEADS_A = 8
N_KV_A = 2
N_HEADS_B = 8
N_KV_B = 2
GROUP = 4
D_FF = 4 * D_MODEL
GRID_W = 64
BLOCK = 128
WINDOW = 128
N_BUCKETS = 32
MAX_DISTANCE = 128
ROPE_THETA = 10000.0
EPS = 1e-6
NEG_INF = -1e30
QA_W = N_HEADS_A * HEAD_DIM
KV_W = N_KV_A * HEAD_DIM
QB_W = N_HEADS_B * HEAD_DIM
IN_TOTAL = QA_W + 2 * KV_W + QB_W + 2 * KV_W
LOG2E = math.log2(math.e)
Q_SCALE = HEAD_DIM ** -0.5 * LOG2E

VMEM_LIMIT_V7X = 56 * 1024 * 1024

PROJ_TS = 512
ATT_TQ = 512
ATT_TK = 512
B_NQB = 4
POST_TS = 512
FF_CHUNK = 1024

_F32 = jnp.float32
_BF16 = jnp.bfloat16


def _dot(a, b):
    return jnp.dot(a, b, preferred_element_type=_F32)


def _dot_nt(a, b):
    return lax.dot_general(a, b, (((1,), (1,)), ((), ())), preferred_element_type=_F32)


def _dot_tn(a, b):
    return lax.dot_general(a, b, (((0,), (0,)), ((), ())), preferred_element_type=_F32)


def _rms_rows(x, g):
    ms = jnp.mean(x * x, axis=-1, keepdims=True)
    return x * lax.rsqrt(ms + EPS) * g


def _rope_partner(t):
    return jnp.concatenate([t[16:32], t[0:16], t[48:64], t[32:48]], axis=0)


def _proj_kernel(x_ref, g_ref, wt_ref, cos_ref, sin_ref, gq_ref, gqp_ref, gk_ref, gkp_ref,
                 qa_ref, ka_ref, va_ref, qb_ref, kb_ref, vb_ref):
    x = x_ref[0]
    h = _rms_rows(x, g_ref[...]).astype(_BF16)
    pt = _dot_nt(wt_ref[...], h)

    cos = cos_ref[...]
    sin = sin_ref[...]

    def norm_rope(blk, coef_a, coef_b):
        ms = jnp.mean(blk * blk, axis=0, keepdims=True)
        t = blk * lax.rsqrt(ms + EPS)
        return t * coef_a + _rope_partner(t) * coef_b

    qa_a = gq_ref[...] * cos * Q_SCALE
    qa_b = gqp_ref[...] * sin * Q_SCALE
    for j in range(N_HEADS_A):
        blk = pt[j * HEAD_DIM:(j + 1) * HEAD_DIM]
        qa_ref[0, j * HEAD_DIM:(j + 1) * HEAD_DIM, :] = norm_rope(blk, qa_a, qa_b).astype(_BF16)

    off = QA_W
    ka_a = gk_ref[...] * cos
    ka_b = gkp_ref[...] * sin
    kat = jnp.concatenate(
        [norm_rope(pt[off + j * HEAD_DIM: off + (j + 1) * HEAD_DIM], ka_a, ka_b) for j in range(N_KV_A)],
        axis=0)
    ka_ref[0] = kat.T.astype(_BF16)

    off += KV_W
    va_ref[0] = pt[off:off + KV_W].astype(_BF16)

    off += KV_W
    qb_ref[0] = (pt[off:off + QB_W] * Q_SCALE).astype(_BF16)

    off += QB_W
    kb_ref[0] = pt[off:off + KV_W].T.astype(_BF16)

    off += KV_W
    vb_ref[0] = pt[off:off + KV_W].astype(_BF16)


def _proj(x, g_pre, wt, cos_t, sin_t, gq, gqp, gk, gkp):
    bsz, s_len, _ = x.shape
    ts = PROJ_TS
    const = lambda b, i: (0, 0)
    fm = lambda b, i: (b, 0, i)
    tm = lambda b, i: (b, i, 0)
    col = pl.BlockSpec((HEAD_DIM, 1), const)
    out_shape = (
        jax.ShapeDtypeStruct((bsz, QA_W, s_len), _BF16),
        jax.ShapeDtypeStruct((bsz, s_len, KV_W), _BF16),
        jax.ShapeDtypeStruct((bsz, KV_W, s_len), _BF16),
        jax.ShapeDtypeStruct((bsz, QB_W, s_len), _BF16),
        jax.ShapeDtypeStruct((bsz, s_len, KV_W), _BF16),
        jax.ShapeDtypeStruct((bsz, KV_W, s_len), _BF16),
    )
    return pl.pallas_call(
        _proj_kernel,
        grid=(bsz, s_len // ts),
        in_specs=[
            pl.BlockSpec((1, ts, D_MODEL), tm),
            pl.BlockSpec((1, D_MODEL), const),
            pl.BlockSpec((IN_TOTAL, D_MODEL), const),
            pl.BlockSpec((HEAD_DIM, ts), lambda b, i: (0, i)),
            pl.BlockSpec((HEAD_DIM, ts), lambda b, i: (0, i)),
            col, col, col, col,
        ],
        out_specs=(
            pl.BlockSpec((1, QA_W, ts), fm),
            pl.BlockSpec((1, ts, KV_W), tm),
            pl.BlockSpec((1, KV_W, ts), fm),
            pl.BlockSpec((1, QB_W, ts), fm),
            pl.BlockSpec((1, ts, KV_W), tm),
            pl.BlockSpec((1, KV_W, ts), fm),
        ),
        out_shape=out_shape,
        compiler_params=pltpu.CompilerParams(
            dimension_semantics=("parallel", "parallel"), vmem_limit_bytes=VMEM_LIMIT_V7X),
        name="proj",
    )(x, g_pre, wt, cos_t, sin_t, gq, gqp, gk, gkp)


def _bias_kernel(table_ref, bucket_ref, out_ref):
    bucket = bucket_ref[...]
    for h in range(N_HEADS_B):
        acc = jnp.zeros(bucket.shape, _F32)
        for k in range(N_BUCKETS):
            acc = jnp.where(bucket == k, table_ref[k, h], acc)
        g, j = divmod(h, GROUP)
        out_ref[g, :, j * BLOCK:(j + 1) * BLOCK] = acc


def _bias(table, bucket_t):
    span = 3 * BLOCK
    return pl.pallas_call(
        _bias_kernel,
        in_specs=[
            pl.BlockSpec(memory_space=pltpu.SMEM),
            pl.BlockSpec((span, BLOCK), lambda: (0, 0)),
        ],
        out_specs=pl.BlockSpec((N_KV_B, span, GROUP * BLOCK), lambda: (0, 0, 0)),
        out_shape=jax.ShapeDtypeStruct((N_KV_B, span, GROUP * BLOCK), _F32),
        name="bias",
    )(table, bucket_t)


def _pad_q(q, g):
    zero = jnp.zeros_like(q)
    return jnp.concatenate([jnp.where(g == 0, q, zero), jnp.where(g == 1, q, zero)], axis=0)


def _attn_a_kernel(q_ref, k_ref, v_ref, o_ref, *, n_chunks, tk):
    g = pl.program_id(1) // GROUP
    qp = _pad_q(q_ref[0], g)
    tq = qp.shape[1]
    ones = jnp.ones((HEAD_DIM, tk), _BF16)

    def body(c, carry):
        m, acc = carry
        start = pl.multiple_of(c * tk, tk)
        s = _dot(k_ref[0, pl.ds(start, tk), :], qp)
        m_new = jnp.maximum(m, jnp.max(s, axis=0, keepdims=True))
        p = jnp.exp2(s - m_new).astype(_BF16)
        v_ext = jnp.concatenate([v_ref[0, :, pl.ds(start, tk)], ones], axis=0)
        pv = _dot(v_ext, p)
        alpha = jnp.exp2(m - m_new)
        return m_new, acc * alpha + pv

    m0 = jnp.full((1, tq), NEG_INF, _F32)
    acc0 = jnp.zeros((2 * HEAD_DIM, tq), _F32)
    _, acc = lax.fori_loop(0, n_chunks, body, (m0, acc0))
    o_ref[0] = (acc[:HEAD_DIM] / acc[HEAD_DIM:HEAD_DIM + 1]).astype(_BF16)


def _attn_a(qt, k, vt):
    bsz, _, s_len = qt.shape
    tq, tk = ATT_TQ, ATT_TK
    kern = functools.partial(_attn_a_kernel, n_chunks=s_len // tk, tk=tk)
    return pl.pallas_call(
        kern,
        grid=(bsz, N_HEADS_A, s_len // tq),
        in_specs=[
            pl.BlockSpec((1, HEAD_DIM, tq), lambda b, h, i: (b, h, i)),
            pl.BlockSpec((1, s_len, KV_W), lambda b, h, i: (b, 0, 0)),
            pl.BlockSpec((1, HEAD_DIM, s_len), lambda b, h, i: (b, h // GROUP, 0)),
        ],
        out_specs=pl.BlockSpec((1, HEAD_DIM, tq), lambda b, h, i: (b, h, i)),
        out_shape=jax.ShapeDtypeStruct((bsz, QA_W, s_len), _BF16),
        compiler_params=pltpu.CompilerParams(
            dimension_semantics=("parallel", "parallel", "parallel"), vmem_limit_bytes=VMEM_LIMIT_V7X),
        name="attn_a",
    )(qt, k, vt)


def _attn_b_kernel(q_ref, k_ref, v_ref, bias_ref, sink_ref, o_ref, *, n_blocks):
    g = pl.program_id(1)
    i = pl.program_id(2)
    span = 3 * BLOCK
    s_len = n_blocks * BLOCK
    bias = bias_ref[0]
    sink = sink_ref[0]
    c_idx = lax.broadcasted_iota(jnp.int32, (span, GROUP * BLOCK), 0)
    a_idx = lax.broadcasted_iota(jnp.int32, (span, GROUP * BLOCK), 1) % BLOCK
    rel = c_idx - BLOCK - a_idx
    band = jnp.abs(rel) <= WINDOW
    ones = jnp.ones((HEAD_DIM, span), _BF16)

    for t in range(B_NQB):
        n = i * B_NQB + t
        q4 = jnp.concatenate(
            [q_ref[0, j * HEAD_DIM:(j + 1) * HEAD_DIM, t * BLOCK:(t + 1) * BLOCK] for j in range(GROUP)],
            axis=1)
        qp = _pad_q(q4, g)
        starts = [pl.multiple_of(jnp.clip(n + d, 0, n_blocks - 1) * BLOCK, BLOCK) for d in (-1, 0, 1)]
        k_span = jnp.concatenate([k_ref[0, pl.ds(st, BLOCK), :] for st in starts], axis=0)
        v_span = jnp.concatenate([v_ref[0, :, pl.ds(st, BLOCK)] for st in starts], axis=1)
        kpos = (n - 1) * BLOCK + c_idx
        valid = band & (kpos >= 0) & (kpos < s_len)
        s = jnp.where(valid, _dot(k_span, qp) + bias, NEG_INF)
        m = jnp.maximum(jnp.max(s, axis=0, keepdims=True), sink)
        p = jnp.exp2(s - m)
        pv = _dot(jnp.concatenate([v_span, ones], axis=0), p.astype(_BF16))
        denom = pv[HEAD_DIM:HEAD_DIM + 1] + jnp.exp2(sink - m)
        o = (pv[:HEAD_DIM] / denom).astype(_BF16)
        for j in range(GROUP):
            o_ref[0, j * HEAD_DIM:(j + 1) * HEAD_DIM, t * BLOCK:(t + 1) * BLOCK] = o[:, j * BLOCK:(j + 1) * BLOCK]


def _attn_b(qt, k, vt, bias_t, sink_row):
    bsz, _, s_len = qt.shape
    n_blocks = s_len // BLOCK
    tq = B_NQB * BLOCK
    gw = GROUP * HEAD_DIM
    kern = functools.partial(_attn_b_kernel, n_blocks=n_blocks)
    return pl.pallas_call(
        kern,
        grid=(bsz, N_KV_B, s_len // tq),
        in_specs=[
            pl.BlockSpec((1, gw, tq), lambda b, g, i: (b, g, i)),
            pl.BlockSpec((1, s_len, KV_W), lambda b, g, i: (b, 0, 0)),
            pl.BlockSpec((1, HEAD_DIM, s_len), lambda b, g, i: (b, g, 0)),
            pl.BlockSpec((1, 3 * BLOCK, GROUP * BLOCK), lambda b, g, i: (g, 0, 0)),
            pl.BlockSpec((1, 1, GROUP * BLOCK), lambda b, g, i: (g, 0, 0)),
        ],
        out_specs=pl.BlockSpec((1, gw, tq), lambda b, g, i: (b, g, i)),
        out_shape=jax.ShapeDtypeStruct((bsz, QB_W, s_len), _BF16),
        compiler_params=pltpu.CompilerParams(
            dimension_semantics=("parallel", "parallel", "parallel"), vmem_limit_bytes=VMEM_LIMIT_V7X),
        name="attn_b",
    )(qt, k, vt, bias_t, sink_row)


def _post_kernel(x_ref, oa_ref, ob_ref, wo_ref, wup_ref, wdn_ref, g1_ref, g2_ref, g3_ref, out_ref):
    mix = _dot_tn(oa_ref[0], wo_ref[0:QA_W, :]) + _dot_tn(ob_ref[0], wo_ref[QA_W:, :])
    x1 = x_ref[0] + _rms_rows(mix, g1_ref[...])
    h = _rms_rows(x1, g2_ref[...]).astype(_BF16)
    f = jnp.zeros_like(x1)
    for c in range(D_FF // FF_CHUNK):
        u = jnp.maximum(_dot(h, wup_ref[:, c * FF_CHUNK:(c + 1) * FF_CHUNK]), 0.0)
        f = f + _dot((u * u).astype(_BF16), wdn_ref[c * FF_CHUNK:(c + 1) * FF_CHUNK, :])
    out_ref[0] = x1 + _rms_rows(f, g3_ref[...])


def _post(x, oa, ob, wo, wup, wdn, g1, g2, g3):
    bsz, s_len, _ = x.shape
    ts = POST_TS
    const = lambda b, i: (0, 0)
    resident = functools.partial(pl.BlockSpec, index_map=const, pipeline_mode=pl.Buffered(1))
    return pl.pallas_call(
        _post_kernel,
        grid=(bsz, s_len // ts),
        in_specs=[
            pl.BlockSpec((1, ts, D_MODEL), lambda b, i: (b, i, 0)),
            pl.BlockSpec((1, QA_W, ts), lambda b, i: (b, 0, i)),
            pl.BlockSpec((1, QB_W, ts), lambda b, i: (b, 0, i)),
            resident((D_MODEL, D_MODEL)),
            resident((D_MODEL, D_FF)),
            resident((D_FF, D_MODEL)),
            pl.BlockSpec((1, D_MODEL), const),
            pl.BlockSpec((1, D_MODEL), const),
            pl.BlockSpec((1, D_MODEL), const),
        ],
        out_specs=pl.BlockSpec((1, ts, D_MODEL), lambda b, i: (b, i, 0)),
        out_shape=jax.ShapeDtypeStruct((bsz, s_len, D_MODEL), _F32),
        compiler_params=pltpu.CompilerParams(
            dimension_semantics=("parallel", "parallel"), vmem_limit_bytes=VMEM_LIMIT_V7X),
        name="post",
    )(x, oa, ob, wo, wup, wdn, g1, g2, g3)


def _rope_tables(s_len):
    nf = HEAD_DIM // 4
    pos = jnp.arange(s_len, dtype=jnp.int32)
    row = (pos // GRID_W).astype(_F32)
    col = (pos % GRID_W).astype(_F32)
    freqs = ROPE_THETA ** (-jnp.arange(nf, dtype=_F32) / nf)
    ang_r = row[:, None] * freqs[None, :]
    ang_c = col[:, None] * freqs[None, :]
    cr, sr, cc, sc = jnp.cos(ang_r), jnp.sin(ang_r), jnp.cos(ang_c), jnp.sin(ang_c)
    cos_t = jnp.concatenate([cr, cr, cc, cc], axis=1).T
    sin_t = jnp.concatenate([-sr, sr, -sc, sc], axis=1).T
    return cos_t, sin_t


def _t5_bucket_t():
    nb = N_BUCKETS // 2
    a = jnp.arange(BLOCK, dtype=jnp.int32)
    c = jnp.arange(3 * BLOCK, dtype=jnp.int32)
    rel = c[:, None] - BLOCK - a[None, :]
    ret = (rel > 0).astype(jnp.int32) * nb
    n = jnp.abs(rel)
    max_exact = nb // 2
    nf = jnp.maximum(n, 1).astype(_F32)
    large = max_exact + (jnp.log(nf / max_exact) / np.float32(np.log(MAX_DISTANCE / max_exact))
                         * (nb - max_exact)).astype(jnp.int32)
    large = jnp.minimum(large, nb - 1)
    return ret + jnp.where(n < max_exact, n, large)


def _partner_gain(g):
    g4 = g.reshape(4, HEAD_DIM // 4)
    return jnp.stack([g4[1], g4[0], g4[3], g4[2]]).reshape(HEAD_DIM, 1)


def kernel(x, w_in, w_o, g_pre_mix, g_post_mix, q_norm_a, k_norm_a, sink_b, rel_bias,
           g_pre_ffn, w_ffn_up, w_ffn_down, g_post_ffn):
    bsz, s_len, _ = x.shape
    assert s_len % max(PROJ_TS, ATT_TQ, ATT_TK, B_NQB * BLOCK, POST_TS) == 0
    cos_t, sin_t = _rope_tables(s_len)
    bucket_t = _t5_bucket_t()
    bias_t = _bias(rel_bias * LOG2E, bucket_t)

    for l in range(w_in.shape[0]):
        gq, gk = q_norm_a[l], k_norm_a[l]
        sink_row = jnp.repeat(sink_b[l] * LOG2E, BLOCK).reshape(N_KV_B, 1, GROUP * BLOCK)
        qa, ka, va, qb, kb, vb = _proj(
            x, g_pre_mix[l][None], w_in[l].T.astype(_BF16), cos_t, sin_t,
            gq.reshape(HEAD_DIM, 1), _partner_gain(gq), gk.reshape(HEAD_DIM, 1), _partner_gain(gk))
        oa = _attn_a(qa, ka, va)
        ob = _attn_b(qb, kb, vb, bias_t, sink_row)
        x = _post(x, oa, ob, w_o[l].astype(_BF16), w_ffn_up[l].astype(_BF16), w_ffn_down[l].astype(_BF16),
                  g_post_mix[l][None], g_pre_ffn[l][None], g_post_ffn[l][None])
    return x
```

```python
import functools
import math

import numpy as np
import jax
import jax.numpy as jnp
from jax import lax
from jax.experimental import pallas as pl
from jax.experimental.pallas import tpu as pltpu

D_MODEL = 1024
HEAD_DIM = 64
N_HEADS_A = 8
N_KV_A = 2
N_HEADS_B = 8
N_KV_B = 2
GROUP = 4
D_FF = 4 * D_MODEL
GRID_W = 64
BLOCK = 128
WINDOW = 128
N_BUCKETS = 32
MAX_DISTANCE = 128
ROPE_THETA = 10000.0
EPS = 1e-6
NEG_INF = -1e30
QA_W = N_HEADS_A * HEAD_DIM
KV_W = N_KV_A * HEAD_DIM
QB_W = N_HEADS_B * HEAD_DIM
IN_TOTAL = QA_W + 2 * KV_W + QB_W + 2 * KV_W
LOG2E = math.log2(math.e)
Q_SCALE = HEAD_DIM ** -0.5 * LOG2E

VMEM_LIMIT_V7X = 56 * 1024 * 1024

PROJ_TS = 512
ATT_TQ = 512
V_EXT_ROWS = HEAD_DIM + 16
B_NQB = 4
POST_TS = 512
FF_CHUNK = 1024

_F32 = jnp.float32
_BF16 = jnp.bfloat16


def _dot(a, b):
    return jnp.dot(a, b, preferred_element_type=_F32)


def _dot_nt(a, b):
    return lax.dot_general(a, b, (((1,), (1,)), ((), ())), preferred_element_type=_F32)


def _dot_tn(a, b):
    return lax.dot_general(a, b, (((0,), (0,)), ((), ())), preferred_element_type=_F32)


def _rms_rows(x, g):
    ms = jnp.mean(x * x, axis=-1, keepdims=True)
    return x * lax.rsqrt(ms + EPS) * g


def _rope_partner(t):
    return jnp.concatenate([t[16:32], t[0:16], t[48:64], t[32:48]], axis=0)


def _proj_kernel(x_ref, g_ref, wt_ref, cos_ref, sin_ref, gq_ref, gqp_ref, gk_ref, gkp_ref,
                 qa_ref, ka_ref, va_ref, qb_ref, kb_ref, vb_ref):
    x = x_ref[0]
    h = _rms_rows(x, g_ref[...]).astype(_BF16)
    pt = _dot_nt(wt_ref[...], h)

    cos = cos_ref[...]
    sin = sin_ref[...]

    def norm_rope(blk, coef_a, coef_b):
        ms = jnp.mean(blk * blk, axis=0, keepdims=True)
        t = blk * lax.rsqrt(ms + EPS)
        return t * coef_a + _rope_partner(t) * coef_b

    qa_a = gq_ref[...] * cos * Q_SCALE
    qa_b = gqp_ref[...] * sin * Q_SCALE
    for j in range(N_HEADS_A):
        blk = pt[j * HEAD_DIM:(j + 1) * HEAD_DIM]
        qa_ref[0, j * HEAD_DIM:(j + 1) * HEAD_DIM, :] = norm_rope(blk, qa_a, qa_b).astype(_BF16)

    off = QA_W
    ka_a = gk_ref[...] * cos
    ka_b = gkp_ref[...] * sin
    kat = jnp.concatenate(
        [norm_rope(pt[off + j * HEAD_DIM: off + (j + 1) * HEAD_DIM], ka_a, ka_b) for j in range(N_KV_A)],
        axis=0)
    ka_ref[0] = kat.T.astype(_BF16)

    off += KV_W
    va_ref[0] = pt[off:off + KV_W].astype(_BF16)

    off += KV_W
    qb_ref[0] = (pt[off:off + QB_W] * Q_SCALE).astype(_BF16)

    off += QB_W
    kb_ref[0] = pt[off:off + KV_W].T.astype(_BF16)

    off += KV_W
    vb_ref[0] = pt[off:off + KV_W].astype(_BF16)


def _proj(x, g_pre, wt, cos_t, sin_t, gq, gqp, gk, gkp):
    bsz, s_len, _ = x.shape
    ts = PROJ_TS
    const = lambda b, i: (0, 0)
    fm = lambda b, i: (b, 0, i)
    tm = lambda b, i: (b, i, 0)
    col = pl.BlockSpec((HEAD_DIM, 1), const)
    out_shape = (
        jax.ShapeDtypeStruct((bsz, QA_W, s_len), _BF16),
        jax.ShapeDtypeStruct((bsz, s_len, KV_W), _BF16),
        jax.ShapeDtypeStruct((bsz, KV_W, s_len), _BF16),
        jax.ShapeDtypeStruct((bsz, QB_W, s_len), _BF16),
        jax.ShapeDtypeStruct((bsz, s_len, KV_W), _BF16),
        jax.ShapeDtypeStruct((bsz, KV_W, s_len), _BF16),
    )
    return pl.pallas_call(
        _proj_kernel,
        grid=(bsz, s_len // ts),
        in_specs=[
            pl.BlockSpec((1, ts, D_MODEL), tm),
            pl.BlockSpec((1, D_MODEL), const),
            pl.BlockSpec((IN_TOTAL, D_MODEL), const),
            pl.BlockSpec((HEAD_DIM, ts), lambda b, i: (0, i)),
            pl.BlockSpec((HEAD_DIM, ts), lambda b, i: (0, i)),
            col, col, col, col,
        ],
        out_specs=(
            pl.BlockSpec((1, QA_W, ts), fm),
            pl.BlockSpec((1, ts, KV_W), tm),
            pl.BlockSpec((1, KV_W, ts), fm),
            pl.BlockSpec((1, QB_W, ts), fm),
            pl.BlockSpec((1, ts, KV_W), tm),
            pl.BlockSpec((1, KV_W, ts), fm),
        ),
        out_shape=out_shape,
        compiler_params=pltpu.CompilerParams(
            dimension_semantics=("parallel", "parallel"), vmem_limit_bytes=VMEM_LIMIT_V7X),
        name="proj",
    )(x, g_pre, wt, cos_t, sin_t, gq, gqp, gk, gkp)


def _bias_kernel(table_ref, bucket_ref, out_ref):
    bucket = bucket_ref[...]
    for h in range(N_HEADS_B):
        acc = jnp.zeros(bucket.shape, _F32)
        for k in range(N_BUCKETS):
            acc = jnp.where(bucket == k, table_ref[k, h], acc)
        g, j = divmod(h, GROUP)
        out_ref[g, :, j * BLOCK:(j + 1) * BLOCK] = acc


def _bias(table, bucket_t):
    span = 3 * BLOCK
    return pl.pallas_call(
        _bias_kernel,
        in_specs=[
            pl.BlockSpec(memory_space=pltpu.SMEM),
            pl.BlockSpec((span, BLOCK), lambda: (0, 0)),
        ],
        out_specs=pl.BlockSpec((N_KV_B, span, GROUP * BLOCK), lambda: (0, 0, 0)),
        out_shape=jax.ShapeDtypeStruct((N_KV_B, span, GROUP * BLOCK), _F32),
        name="bias",
    )(table, bucket_t)


def _pad_q(q, g):
    zero = jnp.zeros_like(q)
    return jnp.concatenate([jnp.where(g == 0, q, zero), jnp.where(g == 1, q, zero)], axis=0)


def _attn_a_kernel(q_ref, k_ref, v_ref, o_ref, v_scr, s0, s1, m0, m1, *, tq):
    g = pl.program_id(1)
    s_len = k_ref.shape[1]
    nq = s_len // tq
    n_tiles = GROUP * nq

    v_scr[0:HEAD_DIM, :] = v_ref[0]
    v_scr[HEAD_DIM:, :] = jnp.ones((V_EXT_ROWS - HEAD_DIM, s_len), _BF16)

    def tile_slices(t):
        j = t // nq
        i = t - j * nq
        return pl.ds(pl.multiple_of(j * HEAD_DIM, HEAD_DIM), HEAD_DIM), pl.ds(pl.multiple_of(i * tq, tq), tq)

    def stage_a(t, s_buf, m_buf):
        rows, cols = tile_slices(t)
        s = _dot(k_ref[0], _pad_q(q_ref[0, rows, cols], g))
        s_buf[...] = s
        m_buf[...] = jnp.max(s, axis=0, keepdims=True)

    def stage_b(t, s_buf, m_buf):
        rows, cols = tile_slices(t)
        p = jnp.exp2(s_buf[...] - m_buf[...]).astype(_BF16)
        acc = _dot(v_scr[...], p)
        o_ref[0, rows, cols] = (acc[:HEAD_DIM] / acc[HEAD_DIM:HEAD_DIM + 1]).astype(_BF16)

    stage_a(0, s0, m0)

    def pair(u, carry):
        t = 2 * u
        stage_a(t + 1, s1, m1)
        stage_b(t, s0, m0)
        stage_a(t + 2, s0, m0)
        stage_b(t + 1, s1, m1)
        return carry

    lax.fori_loop(0, n_tiles // 2 - 1, pair, 0)
    stage_a(n_tiles - 1, s1, m1)
    stage_b(n_tiles - 2, s0, m0)
    stage_b(n_tiles - 1, s1, m1)


def _attn_a(qt, k, vt):
    bsz, _, s_len = qt.shape
    tq = ATT_TQ
    gw = GROUP * HEAD_DIM
    kern = functools.partial(_attn_a_kernel, tq=tq)
    return pl.pallas_call(
        kern,
        grid=(bsz, N_KV_A),
        in_specs=[
            pl.BlockSpec((1, gw, s_len), lambda b, g: (b, g, 0)),
            pl.BlockSpec((1, s_len, KV_W), lambda b, g: (b, 0, 0)),
            pl.BlockSpec((1, HEAD_DIM, s_len), lambda b, g: (b, g, 0)),
        ],
        out_specs=pl.BlockSpec((1, gw, s_len), lambda b, g: (b, g, 0)),
        out_shape=jax.ShapeDtypeStruct((bsz, QA_W, s_len), _BF16),
        scratch_shapes=[
            pltpu.VMEM((V_EXT_ROWS, s_len), _BF16),
            pltpu.VMEM((s_len, tq), _F32),
            pltpu.VMEM((s_len, tq), _F32),
            pltpu.VMEM((1, tq), _F32),
            pltpu.VMEM((1, tq), _F32),
        ],
        compiler_params=pltpu.CompilerParams(
            dimension_semantics=("parallel", "parallel"), vmem_limit_bytes=VMEM_LIMIT_V7X),
        name="attn_a",
    )(qt, k, vt)


def _attn_b_kernel(q_ref, k_ref, v_ref, bias_ref, sink_ref, o_ref, *, n_blocks):
    g = pl.program_id(1)
    i = pl.program_id(2)
    span = 3 * BLOCK
    s_len = n_blocks * BLOCK
    bias = bias_ref[0]
    sink = sink_ref[0]
    c_idx = lax.broadcasted_iota(jnp.int32, (span, GROUP * BLOCK), 0)
    a_idx = lax.broadcasted_iota(jnp.int32, (span, GROUP * BLOCK), 1) % BLOCK
    rel = c_idx - BLOCK - a_idx
    band = jnp.abs(rel) <= WINDOW
    ones = jnp.ones((HEAD_DIM, span), _BF16)

    for t in range(B_NQB):
        n = i * B_NQB + t
        q4 = jnp.concatenate(
            [q_ref[0, j * HEAD_DIM:(j + 1) * HEAD_DIM, t * BLOCK:(t + 1) * BLOCK] for j in range(GROUP)],
            axis=1)
        qp = _pad_q(q4, g)
        starts = [pl.multiple_of(jnp.clip(n + d, 0, n_blocks - 1) * BLOCK, BLOCK) for d in (-1, 0, 1)]
        k_span = jnp.concatenate([k_ref[0, pl.ds(st, BLOCK), :] for st in starts], axis=0)
        v_span = jnp.concatenate([v_ref[0, :, pl.ds(st, BLOCK)] for st in starts], axis=1)
        kpos = (n - 1) * BLOCK + c_idx
        valid = band & (kpos >= 0) & (kpos < s_len)
        s = jnp.where(valid, _dot(k_span, qp) + bias, NEG_INF)
        m = jnp.maximum(jnp.max(s, axis=0, keepdims=True), sink)
        p = jnp.exp2(s - m)
        pv = _dot(jnp.concatenate([v_span, ones], axis=0), p.astype(_BF16))
        denom = pv[HEAD_DIM:HEAD_DIM + 1] + jnp.exp2(sink - m)
        o = (pv[:HEAD_DIM] / denom).astype(_BF16)
        for j in range(GROUP):
            o_ref[0, j * HEAD_DIM:(j + 1) * HEAD_DIM, t * BLOCK:(t + 1) * BLOCK] = o[:, j * BLOCK:(j + 1) * BLOCK]


def _attn_b(qt, k, vt, bias_t, sink_row):
    bsz, _, s_len = qt.shape
    n_blocks = s_len // BLOCK
    tq = B_NQB * BLOCK
    gw = GROUP * HEAD_DIM
    kern = functools.partial(_attn_b_kernel, n_blocks=n_blocks)
    return pl.pallas_call(
        kern,
        grid=(bsz, N_KV_B, s_len // tq),
        in_specs=[
            pl.BlockSpec((1, gw, tq), lambda b, g, i: (b, g, i)),
            pl.BlockSpec((1, s_len, KV_W), lambda b, g, i: (b, 0, 0)),
            pl.BlockSpec((1, HEAD_DIM, s_len), lambda b, g, i: (b, g, 0)),
            pl.BlockSpec((1, 3 * BLOCK, GROUP * BLOCK), lambda b, g, i: (g, 0, 0)),
            pl.BlockSpec((1, 1, GROUP * BLOCK), lambda b, g, i: (g, 0, 0)),
        ],
        out_specs=pl.BlockSpec((1, gw, tq), lambda b, g, i: (b, g, i)),
        out_shape=jax.ShapeDtypeStruct((bsz, QB_W, s_len), _BF16),
        compiler_params=pltpu.CompilerParams(
            dimension_semantics=("parallel", "parallel", "parallel"), vmem_limit_bytes=VMEM_LIMIT_V7X),
        name="attn_b",
    )(qt, k, vt, bias_t, sink_row)


def _post_kernel(x_ref, oa_ref, ob_ref, wo_ref, wup_ref, wdn_ref, g1_ref, g2_ref, g3_ref, out_ref):
    mix = _dot_tn(oa_ref[0], wo_ref[0:QA_W, :]) + _dot_tn(ob_ref[0], wo_ref[QA_W:, :])
    x1 = x_ref[0] + _rms_rows(mix, g1_ref[...])
    h = _rms_rows(x1, g2_ref[...]).astype(_BF16)
    f = jnp.zeros_like(x1)
    for c in range(D_FF // FF_CHUNK):
        u = jnp.maximum(_dot(h, wup_ref[:, c * FF_CHUNK:(c + 1) * FF_CHUNK]), 0.0)
        f = f + _dot((u * u).astype(_BF16), wdn_ref[c * FF_CHUNK:(c + 1) * FF_CHUNK, :])
    out_ref[0] = x1 + _rms_rows(f, g3_ref[...])


def _post(x, oa, ob, wo, wup, wdn, g1, g2, g3):
    bsz, s_len, _ = x.shape
    ts = POST_TS
    const = lambda b, i: (0, 0)
    resident = functools.partial(pl.BlockSpec, index_map=const, pipeline_mode=pl.Buffered(1))
    return pl.pallas_call(
        _post_kernel,
        grid=(bsz, s_len // ts),
        in_specs=[
            pl.BlockSpec((1, ts, D_MODEL), lambda b, i: (b, i, 0)),
            pl.BlockSpec((1, QA_W, ts), lambda b, i: (b, 0, i)),
            pl.BlockSpec((1, QB_W, ts), lambda b, i: (b, 0, i)),
            resident((D_MODEL, D_MODEL)),
            resident((D_MODEL, D_FF)),
            resident((D_FF, D_MODEL)),
            pl.BlockSpec((1, D_MODEL), const),
            pl.BlockSpec((1, D_MODEL), const),
            pl.BlockSpec((1, D_MODEL), const),
        ],
        out_specs=pl.BlockSpec((1, ts, D_MODEL), lambda b, i: (b, i, 0)),
        out_shape=jax.ShapeDtypeStruct((bsz, s_len, D_MODEL), _F32),
        compiler_params=pltpu.CompilerParams(
            dimension_semantics=("parallel", "parallel"), vmem_limit_bytes=VMEM_LIMIT_V7X),
        name="post",
    )(x, oa, ob, wo, wup, wdn, g1, g2, g3)


def _rope_tables(s_len):
    nf = HEAD_DIM // 4
    pos = jnp.arange(s_len, dtype=jnp.int32)
    row = (pos // GRID_W).astype(_F32)
    col = (pos % GRID_W).astype(_F32)
    freqs = ROPE_THETA ** (-jnp.arange(nf, dtype=_F32) / nf)
    ang_r = row[:, None] * freqs[None, :]
    ang_c = col[:, None] * freqs[None, :]
    cr, sr, cc, sc = jnp.cos(ang_r), jnp.sin(ang_r), jnp.cos(ang_c), jnp.sin(ang_c)
    cos_t = jnp.concatenate([cr, cr, cc, cc], axis=1).T
    sin_t = jnp.concatenate([-sr, sr, -sc, sc], axis=1).T
    return cos_t, sin_t


def _t5_bucket_t():
    nb = N_BUCKETS // 2
    a = jnp.arange(BLOCK, dtype=jnp.int32)
    c = jnp.arange(3 * BLOCK, dtype=jnp.int32)
    rel = c[:, None] - BLOCK - a[None, :]
    ret = (rel > 0).astype(jnp.int32) * nb
    n = jnp.abs(rel)
    max_exact = nb // 2
    nf = jnp.maximum(n, 1).astype(_F32)
    large = max_exact + (jnp.log(nf / max_exact) / np.float32(np.log(MAX_DISTANCE / max_exact))
                         * (nb - max_exact)).astype(jnp.int32)
    large = jnp.minimum(large, nb - 1)
    return ret + jnp.where(n < max_exact, n, large)


def _partner_gain(g):
    g4 = g.reshape(4, HEAD_DIM // 4)
    return jnp.stack([g4[1], g4[0], g4[3], g4[2]]).reshape(HEAD_DIM, 1)


def kernel(x, w_in, w_o, g_pre_mix, g_post_mix, q_norm_a, k_norm_a, sink_b, rel_bias,
           g_pre_ffn, w_ffn_up, w_ffn_down, g_post_ffn):
    bsz, s_len, _ = x.shape
    assert s_len % max(PROJ_TS, ATT_TQ, B_NQB * BLOCK, POST_TS) == 0 and s_len // ATT_TQ >= 1
    cos_t, sin_t = _rope_tables(s_len)
    bucket_t = _t5_bucket_t()
    bias_t = _bias(rel_bias * LOG2E, bucket_t)

    for l in range(w_in.shape[0]):
        gq, gk = q_norm_a[l], k_norm_a[l]
        sink_row = jnp.repeat(sink_b[l] * LOG2E, BLOCK).reshape(N_KV_B, 1, GROUP * BLOCK)
        qa, ka, va, qb, kb, vb = _proj(
            x, g_pre_mix[l][None], w_in[l].T.astype(_BF16), cos_t, sin_t,
            gq.reshape(HEAD_DIM, 1), _partner_gain(gq), gk.reshape(HEAD_DIM, 1), _partner_gain(gk))
        oa = _attn_a(qa, ka, va)
        ob = _attn_b(qb, kb, vb, bias_t, sink_row)
        x = _post(x, oa, ob, w_o[l].astype(_BF16), w_ffn_up[l].astype(_BF16), w_ffn_down[l].astype(_BF16),
                  g_post_mix[l][None], g_pre_ffn[l][None], g_post_ffn[l][None])
    return x
```

```python
import functools
import math

import numpy as np
import jax
import jax.numpy as jnp
from jax import lax
from jax.experimental import pallas as pl
from jax.experimental.pallas import tpu as pltpu

D_MODEL = 1024
HEAD_DIM = 64
N_HEADS_A = 8
N_KV_A = 2
N_HEADS_B = 8
N_KV_B = 2
GROUP = 4
D_FF = 4 * D_MODEL
GRID_W = 64
BLOCK = 128
WINDOW = 128
N_BUCKETS = 32
MAX_DISTANCE = 128
ROPE_THETA = 10000.0
EPS = 1e-6
NEG_INF = -1e30
QA_W = N_HEADS_A * HEAD_DIM
KV_W = N_KV_A * HEAD_DIM
QB_W = N_HEADS_B * HEAD_DIM
IN_TOTAL = QA_W + 2 * KV_W + QB_W + 2 * KV_W
LOG2E = math.log2(math.e)
Q_SCALE = HEAD_DIM ** -0.5 * LOG2E

VMEM_LIMIT_V7X = 56 * 1024 * 1024

PROJ_TS = 512
ATT_TQ = 512
ATT_TK = 256
V_EXT_ROWS = HEAD_DIM + 16
B_UNROLL = 8
POST_TS = 512
FF_CHUNK = 1024

_F32 = jnp.float32
_BF16 = jnp.bfloat16


def _dot(a, b):
    return jnp.dot(a, b, preferred_element_type=_F32)


def _dot_nt(a, b):
    return lax.dot_general(a, b, (((1,), (1,)), ((), ())), preferred_element_type=_F32)


def _dot_tn(a, b):
    return lax.dot_general(a, b, (((0,), (0,)), ((), ())), preferred_element_type=_F32)


def _rms_rows(x, g):
    ms = jnp.mean(x * x, axis=-1, keepdims=True)
    return x * lax.rsqrt(ms + EPS) * g


def _rope_partner(t):
    return jnp.concatenate([t[16:32], t[0:16], t[48:64], t[32:48]], axis=0)


def _proj_kernel(x_ref, g_ref, wt_ref, cos_ref, sin_ref, gq_ref, gqp_ref, gk_ref, gkp_ref,
                 qa_ref, ka_ref, va_ref, qb_ref, kb_ref, vb_ref):
    x = x_ref[0]
    h = _rms_rows(x, g_ref[...]).astype(_BF16)
    pt = _dot_nt(wt_ref[...], h)

    cos = cos_ref[...]
    sin = sin_ref[...]

    def norm_rope(blk, coef_a, coef_b):
        ms = jnp.mean(blk * blk, axis=0, keepdims=True)
        t = blk * lax.rsqrt(ms + EPS)
        return t * coef_a + _rope_partner(t) * coef_b

    qa_a = gq_ref[...] * cos * Q_SCALE
    qa_b = gqp_ref[...] * sin * Q_SCALE
    for j in range(N_HEADS_A):
        blk = pt[j * HEAD_DIM:(j + 1) * HEAD_DIM]
        qa_ref[0, j * HEAD_DIM:(j + 1) * HEAD_DIM, :] = norm_rope(blk, qa_a, qa_b).astype(_BF16)

    off = QA_W
    ka_a = gk_ref[...] * cos
    ka_b = gkp_ref[...] * sin
    kat = jnp.concatenate(
        [norm_rope(pt[off + j * HEAD_DIM: off + (j + 1) * HEAD_DIM], ka_a, ka_b) for j in range(N_KV_A)],
        axis=0)
    ka_ref[0] = kat.T.astype(_BF16)

    off += KV_W
    va_ref[0] = pt[off:off + KV_W].astype(_BF16)

    off += KV_W
    qb_ref[0] = (pt[off:off + QB_W] * Q_SCALE).astype(_BF16)

    off += QB_W
    kb_ref[0] = pt[off:off + KV_W].T.astype(_BF16)

    off += KV_W
    vb_ref[0] = pt[off:off + KV_W].astype(_BF16)


def _proj(x, g_pre, wt, cos_t, sin_t, gq, gqp, gk, gkp):
    bsz, s_len, _ = x.shape
    ts = PROJ_TS
    const = lambda b, i: (0, 0)
    fm = lambda b, i: (b, 0, i)
    tm = lambda b, i: (b, i, 0)
    col = pl.BlockSpec((HEAD_DIM, 1), const)
    out_shape = (
        jax.ShapeDtypeStruct((bsz, QA_W, s_len), _BF16),
        jax.ShapeDtypeStruct((bsz, s_len, KV_W), _BF16),
        jax.ShapeDtypeStruct((bsz, KV_W, s_len), _BF16),
        jax.ShapeDtypeStruct((bsz, QB_W, s_len), _BF16),
        jax.ShapeDtypeStruct((bsz, s_len, KV_W), _BF16),
        jax.ShapeDtypeStruct((bsz, KV_W, s_len), _BF16),
    )
    return pl.pallas_call(
        _proj_kernel,
        grid=(bsz, s_len // ts),
        in_specs=[
            pl.BlockSpec((1, ts, D_MODEL), tm),
            pl.BlockSpec((1, D_MODEL), const),
            pl.BlockSpec((IN_TOTAL, D_MODEL), const),
            pl.BlockSpec((HEAD_DIM, ts), lambda b, i: (0, i)),
            pl.BlockSpec((HEAD_DIM, ts), lambda b, i: (0, i)),
            col, col, col, col,
        ],
        out_specs=(
            pl.BlockSpec((1, QA_W, ts), fm),
            pl.BlockSpec((1, ts, KV_W), tm),
            pl.BlockSpec((1, KV_W, ts), fm),
            pl.BlockSpec((1, QB_W, ts), fm),
            pl.BlockSpec((1, ts, KV_W), tm),
            pl.BlockSpec((1, KV_W, ts), fm),
        ),
        out_shape=out_shape,
        compiler_params=pltpu.CompilerParams(
            dimension_semantics=("parallel", "parallel"), vmem_limit_bytes=VMEM_LIMIT_V7X),
        name="proj",
    )(x, g_pre, wt, cos_t, sin_t, gq, gqp, gk, gkp)


def _bias_kernel(table_ref, bucket_ref, out_ref):
    bucket = bucket_ref[...]
    c_idx = lax.broadcasted_iota(jnp.int32, bucket.shape, 0)
    a_idx = lax.broadcasted_iota(jnp.int32, bucket.shape, 1)
    band = jnp.abs(c_idx - BLOCK - a_idx) <= WINDOW
    edge_valid = (None, c_idx >= BLOCK, c_idx < 2 * BLOCK)
    for h in range(N_HEADS_B):
        acc = jnp.zeros(bucket.shape, _F32)
        for k in range(N_BUCKETS):
            acc = jnp.where(bucket == k, table_ref[k, h], acc)
        g, j = divmod(h, GROUP)
        for e, ok in enumerate(edge_valid):
            valid = band if ok is None else band & ok
            out_ref[g, e, :, j * BLOCK:(j + 1) * BLOCK] = jnp.where(valid, acc, NEG_INF)


def _bias(table, bucket_t):
    span = 3 * BLOCK
    shape = (N_KV_B, 3, span, GROUP * BLOCK)
    return pl.pallas_call(
        _bias_kernel,
        in_specs=[
            pl.BlockSpec(memory_space=pltpu.SMEM),
            pl.BlockSpec((span, BLOCK), lambda: (0, 0)),
        ],
        out_specs=pl.BlockSpec(shape, lambda: (0, 0, 0, 0)),
        out_shape=jax.ShapeDtypeStruct(shape, _F32),
        name="bias",
    )(table, bucket_t)


def _pad_q(q, g):
    zero = jnp.zeros_like(q)
    return jnp.concatenate([jnp.where(g == 0, q, zero), jnp.where(g == 1, q, zero)], axis=0)


def _attn_a_kernel(q_ref, k_ref, v_ref, o_ref, v_scr, s0, s1, m0, m1, *, tq, tk):
    g = pl.program_id(1)
    s_len = k_ref.shape[1]
    nq = s_len // tq
    n_tiles = GROUP * nq

    v_scr[0:HEAD_DIM, :] = v_ref[0]
    v_scr[HEAD_DIM:, :] = jnp.ones((V_EXT_ROWS - HEAD_DIM, s_len), _BF16)

    def tile_slices(t):
        j = t // nq
        i = t - j * nq
        return pl.ds(pl.multiple_of(j * HEAD_DIM, HEAD_DIM), HEAD_DIM), pl.ds(pl.multiple_of(i * tq, tq), tq)

    n_chunks = s_len // tk

    def step(t_a, a_bufs, t_b, b_bufs):
        if t_a is not None:
            rows_a, cols_a = tile_slices(t_a)
            qp = _pad_q(q_ref[0, rows_a, cols_a], g)
            mx = None
        if t_b is not None:
            m_b = b_bufs[1][...]
            acc = jnp.zeros((V_EXT_ROWS, tq), _F32)
        for c in range(n_chunks):
            keys = slice(c * tk, (c + 1) * tk)
            if t_a is not None:
                s = _dot(k_ref[0, keys, :], qp)
                a_bufs[0][keys, :] = s
                cm = jnp.max(s.reshape(tk // 8, 8, tq), axis=0)
                mx = cm if mx is None else jnp.maximum(mx, cm)
            if t_b is not None:
                p = jnp.exp2(b_bufs[0][keys, :] - m_b).astype(_BF16)
                acc = acc + _dot(v_scr[:, keys], p)
        if t_a is not None:
            a_bufs[1][...] = jnp.max(mx, axis=0, keepdims=True)
        if t_b is not None:
            rows_b, cols_b = tile_slices(t_b)
            o_ref[0, rows_b, cols_b] = (acc[:HEAD_DIM] / acc[HEAD_DIM:HEAD_DIM + 1]).astype(_BF16)

    buf0, buf1 = (s0, m0), (s1, m1)
    step(0, buf0, None, None)

    def pair(u, carry):
        t = 2 * u
        step(t + 1, buf1, t, buf0)
        step(t + 2, buf0, t + 1, buf1)
        return carry

    lax.fori_loop(0, n_tiles // 2 - 1, pair, 0)
    step(n_tiles - 1, buf1, n_tiles - 2, buf0)
    step(None, None, n_tiles - 1, buf1)


def _attn_a(qt, k, vt):
    bsz, _, s_len = qt.shape
    tq = ATT_TQ
    gw = GROUP * HEAD_DIM
    kern = functools.partial(_attn_a_kernel, tq=tq, tk=ATT_TK)
    return pl.pallas_call(
        kern,
        grid=(bsz, N_KV_A),
        in_specs=[
            pl.BlockSpec((1, gw, s_len), lambda b, g: (b, g, 0)),
            pl.BlockSpec((1, s_len, KV_W), lambda b, g: (b, 0, 0)),
            pl.BlockSpec((1, HEAD_DIM, s_len), lambda b, g: (b, g, 0)),
        ],
        out_specs=pl.BlockSpec((1, gw, s_len), lambda b, g: (b, g, 0)),
        out_shape=jax.ShapeDtypeStruct((bsz, QA_W, s_len), _BF16),
        scratch_shapes=[
            pltpu.VMEM((V_EXT_ROWS, s_len), _BF16),
            pltpu.VMEM((s_len, tq), _F32),
            pltpu.VMEM((s_len, tq), _F32),
            pltpu.VMEM((1, tq), _F32),
            pltpu.VMEM((1, tq), _F32),
        ],
        compiler_params=pltpu.CompilerParams(
            dimension_semantics=("parallel", "parallel"), vmem_limit_bytes=VMEM_LIMIT_V7X),
        name="attn_a",
    )(qt, k, vt)


def _attn_b_kernel(q_ref, k_ref, v_ref, bias_ref, sink_ref, o_ref, v_scr, s0, s1, m0, m1, *, n_blocks):
    g = pl.program_id(1)
    s_len = n_blocks * BLOCK
    v_scr[0:HEAD_DIM, :] = v_ref[0]
    v_scr[HEAD_DIM:, :] = jnp.ones((V_EXT_ROWS - HEAD_DIM, s_len), _BF16)
    sink = sink_ref[0]

    def span_starts(n):
        return [pl.multiple_of(jnp.clip(n + d, 0, n_blocks - 1) * BLOCK, BLOCK) for d in (-1, 0, 1)]

    def stage_a(n, s_buf, m_buf):
        cols = pl.ds(pl.multiple_of(n * BLOCK, BLOCK), BLOCK)
        q4 = jnp.concatenate(
            [q_ref[0, j * HEAD_DIM:(j + 1) * HEAD_DIM, cols] for j in range(GROUP)], axis=1)
        k_span = jnp.concatenate([k_ref[0, pl.ds(st, BLOCK), :] for st in span_starts(n)], axis=0)
        edge = jnp.where(n == 0, 1, jnp.where(n == n_blocks - 1, 2, 0))
        s = _dot(k_span, _pad_q(q4, g)) + bias_ref[0, edge]
        s_buf[...] = s
        m_buf[...] = jnp.maximum(jnp.max(s, axis=0, keepdims=True), sink)

    def stage_b(n, s_buf, m_buf):
        cols = pl.ds(pl.multiple_of(n * BLOCK, BLOCK), BLOCK)
        m = m_buf[...]
        p = jnp.exp2(s_buf[...] - m).astype(_BF16)
        v_span = jnp.concatenate([v_scr[:, pl.ds(st, BLOCK)] for st in span_starts(n)], axis=1)
        pv = _dot(v_span, p)
        denom = pv[HEAD_DIM:HEAD_DIM + 1] + jnp.exp2(sink - m)
        o = (pv[:HEAD_DIM] / denom).astype(_BF16)
        for j in range(GROUP):
            o_ref[0, j * HEAD_DIM:(j + 1) * HEAD_DIM, cols] = o[:, j * BLOCK:(j + 1) * BLOCK]

    bufs = ((s0, m0), (s1, m1))

    def steps(n0, count):
        for i in range(count):
            stage_a(n0 + i + 1, *bufs[(i + 1) % 2])
            stage_b(n0 + i, *bufs[i % 2])

    stage_a(0, *bufs[0])
    n_loop = (n_blocks - 1) // B_UNROLL

    def body(u, carry):
        steps(u * B_UNROLL, B_UNROLL)
        return carry

    lax.fori_loop(0, n_loop, body, 0)
    done = n_loop * B_UNROLL
    steps(done, n_blocks - 1 - done)
    stage_b(n_blocks - 1, *bufs[(n_blocks - 1) % 2])


def _attn_b(qt, k, vt, bias_t, sink_row):
    bsz, _, s_len = qt.shape
    n_blocks = s_len // BLOCK
    gw = GROUP * HEAD_DIM
    kern = functools.partial(_attn_b_kernel, n_blocks=n_blocks)
    return pl.pallas_call(
        kern,
        grid=(bsz, N_KV_B),
        in_specs=[
            pl.BlockSpec((1, gw, s_len), lambda b, g: (b, g, 0)),
            pl.BlockSpec((1, s_len, KV_W), lambda b, g: (b, 0, 0)),
            pl.BlockSpec((1, HEAD_DIM, s_len), lambda b, g: (b, g, 0)),
            pl.BlockSpec((1, 3, 3 * BLOCK, GROUP * BLOCK), lambda b, g: (g, 0, 0, 0)),
            pl.BlockSpec((1, 1, GROUP * BLOCK), lambda b, g: (g, 0, 0)),
        ],
        out_specs=pl.BlockSpec((1, gw, s_len), lambda b, g: (b, g, 0)),
        out_shape=jax.ShapeDtypeStruct((bsz, QB_W, s_len), _BF16),
        scratch_shapes=[
            pltpu.VMEM((V_EXT_ROWS, s_len), _BF16),
            pltpu.VMEM((3 * BLOCK, GROUP * BLOCK), _F32),
            pltpu.VMEM((3 * BLOCK, GROUP * BLOCK), _F32),
            pltpu.VMEM((1, GROUP * BLOCK), _F32),
            pltpu.VMEM((1, GROUP * BLOCK), _F32),
        ],
        compiler_params=pltpu.CompilerParams(
            dimension_semantics=("parallel", "parallel"), vmem_limit_bytes=VMEM_LIMIT_V7X),
        name="attn_b",
    )(qt, k, vt, bias_t, sink_row)


def _post_kernel(x_ref, oa_ref, ob_ref, wo_ref, wup_ref, wdn_ref, g1_ref, g2_ref, g3_ref, out_ref):
    mix = _dot_tn(oa_ref[0], wo_ref[0:QA_W, :]) + _dot_tn(ob_ref[0], wo_ref[QA_W:, :])
    x1 = x_ref[0] + _rms_rows(mix, g1_ref[...])
    h = _rms_rows(x1, g2_ref[...]).astype(_BF16)
    f = jnp.zeros_like(x1)
    for c in range(D_FF // FF_CHUNK):
        u = jnp.maximum(_dot(h, wup_ref[:, c * FF_CHUNK:(c + 1) * FF_CHUNK]), 0.0)
        f = f + _dot((u * u).astype(_BF16), wdn_ref[c * FF_CHUNK:(c + 1) * FF_CHUNK, :])
    out_ref[0] = x1 + _rms_rows(f, g3_ref[...])


def _post(x, oa, ob, wo, wup, wdn, g1, g2, g3):
    bsz, s_len, _ = x.shape
    ts = POST_TS
    const = lambda b, i: (0, 0)
    resident = functools.partial(pl.BlockSpec, index_map=const, pipeline_mode=pl.Buffered(1))
    return pl.pallas_call(
        _post_kernel,
        grid=(bsz, s_len // ts),
        in_specs=[
            pl.BlockSpec((1, ts, D_MODEL), lambda b, i: (b, i, 0)),
            pl.BlockSpec((1, QA_W, ts), lambda b, i: (b, 0, i)),
            pl.BlockSpec((1, QB_W, ts), lambda b, i: (b, 0, i)),
            resident((D_MODEL, D_MODEL)),
            resident((D_MODEL, D_FF)),
            resident((D_FF, D_MODEL)),
            pl.BlockSpec((1, D_MODEL), const),
            pl.BlockSpec((1, D_MODEL), const),
            pl.BlockSpec((1, D_MODEL), const),
        ],
        out_specs=pl.BlockSpec((1, ts, D_MODEL), lambda b, i: (b, i, 0)),
        out_shape=jax.ShapeDtypeStruct((bsz, s_len, D_MODEL), _F32),
        compiler_params=pltpu.CompilerParams(
            dimension_semantics=("parallel", "parallel"), vmem_limit_bytes=VMEM_LIMIT_V7X),
        name="post",
    )(x, oa, ob, wo, wup, wdn, g1, g2, g3)


def _rope_tables(s_len):
    nf = HEAD_DIM // 4
    pos = jnp.arange(s_len, dtype=jnp.int32)
    row = (pos // GRID_W).astype(_F32)
    col = (pos % GRID_W).astype(_F32)
    freqs = ROPE_THETA ** (-jnp.arange(nf, dtype=_F32) / nf)
    ang_r = row[:, None] * freqs[None, :]
    ang_c = col[:, None] * freqs[None, :]
    cr, sr, cc, sc = jnp.cos(ang_r), jnp.sin(ang_r), jnp.cos(ang_c), jnp.sin(ang_c)
    cos_t = jnp.concatenate([cr, cr, cc, cc], axis=1).T
    sin_t = jnp.concatenate([-sr, sr, -sc, sc], axis=1).T
    return cos_t, sin_t


def _t5_bucket_t():
    nb = N_BUCKETS // 2
    a = jnp.arange(BLOCK, dtype=jnp.int32)
    c = jnp.arange(3 * BLOCK, dtype=jnp.int32)
    rel = c[:, None] - BLOCK - a[None, :]
    ret = (rel > 0).astype(jnp.int32) * nb
    n = jnp.abs(rel)
    max_exact = nb // 2
    nf = jnp.maximum(n, 1).astype(_F32)
    large = max_exact + (jnp.log(nf / max_exact) / np.float32(np.log(MAX_DISTANCE / max_exact))
                         * (nb - max_exact)).astype(jnp.int32)
    large = jnp.minimum(large, nb - 1)
    return ret + jnp.where(n < max_exact, n, large)


def _partner_gain(g):
    g4 = g.reshape(4, HEAD_DIM // 4)
    return jnp.stack([g4[1], g4[0], g4[3], g4[2]]).reshape(HEAD_DIM, 1)


def kernel(x, w_in, w_o, g_pre_mix, g_post_mix, q_norm_a, k_norm_a, sink_b, rel_bias,
           g_pre_ffn, w_ffn_up, w_ffn_down, g_post_ffn):
    bsz, s_len, _ = x.shape
    assert s_len % max(PROJ_TS, ATT_TQ, ATT_TK, POST_TS) == 0 and B_UNROLL % 2 == 0
    cos_t, sin_t = _rope_tables(s_len)
    bucket_t = _t5_bucket_t()
    bias_t = _bias(rel_bias * LOG2E, bucket_t)

    for l in range(w_in.shape[0]):
        gq, gk = q_norm_a[l], k_norm_a[l]
        sink_row = jnp.repeat(sink_b[l] * LOG2E, BLOCK).reshape(N_KV_B, 1, GROUP * BLOCK)
        qa, ka, va, qb, kb, vb = _proj(
            x, g_pre_mix[l][None], w_in[l].T.astype(_BF16), cos_t, sin_t,
            gq.reshape(HEAD_DIM, 1), _partner_gain(gq), gk.reshape(HEAD_DIM, 1), _partner_gain(gk))
        oa = _attn_a(qa, ka, va)
        ob = _attn_b(qb, kb, vb, bias_t, sink_row)
        x = _post(x, oa, ob, w_o[l].astype(_BF16), w_ffn_up[l].astype(_BF16), w_ffn_down[l].astype(_BF16),
                  g_post_mix[l][None], g_pre_ffn[l][None], g_post_ffn[l][None])
    return x
```

```python
import functools
import math

import numpy as np
import jax
import jax.numpy as jnp
from jax import lax
from jax.experimental import pallas as pl
from jax.experimental.pallas import tpu as pltpu

D_MODEL = 1024
HEAD_DIM = 64
N_HEADS_A = 8
N_KV_A = 2
N_HEADS_B = 8
N_KV_B = 2
GROUP = 4
D_FF = 4 * D_MODEL
GRID_W = 64
BLOCK = 128
WINDOW = 128
N_BUCKETS = 32
MAX_DISTANCE = 128
ROPE_THETA = 10000.0
EPS = 1e-6
NEG_INF = -1e30
QA_W = N_HEADS_A * HEAD_DIM
KV_W = N_KV_A * HEAD_DIM
QB_W = N_HEADS_B * HEAD_DIM
IN_TOTAL = QA_W + 2 * KV_W + QB_W + 2 * KV_W
LOG2E = math.log2(math.e)
Q_SCALE = HEAD_DIM ** -0.5 * LOG2E

VMEM_LIMIT_V7X = 56 * 1024 * 1024

PROJ_TS = 1024
ATT_TQ = 512
ATT_TK = 256
V_EXT_ROWS = HEAD_DIM + 16
B_UNROLL = 8
POST_TS = 512
FF_CHUNK = 1024

_F32 = jnp.float32
_BF16 = jnp.bfloat16


def _dot(a, b):
    return jnp.dot(a, b, preferred_element_type=_F32)


def _dot_nt(a, b):
    return lax.dot_general(a, b, (((1,), (1,)), ((), ())), preferred_element_type=_F32)


def _dot_tn(a, b):
    return lax.dot_general(a, b, (((0,), (0,)), ((), ())), preferred_element_type=_F32)


def _rms_rows(x, g):
    ms = jnp.mean(x * x, axis=-1, keepdims=True)
    return x * lax.rsqrt(ms + EPS) * g


def _rope_partner(t):
    return jnp.concatenate([t[16:32], t[0:16], t[48:64], t[32:48]], axis=0)


def _proj_kernel(x_ref, g_ref, wt_ref, cos_ref, sin_ref, gq_ref, gqp_ref, gk_ref, gkp_ref,
                 qa_ref, ka_ref, va_ref, qb_ref, kb_ref, vb_ref):
    x = x_ref[0]
    h = _rms_rows(x, g_ref[...]).astype(_BF16)
    pt = _dot_nt(wt_ref[...], h)

    cos = cos_ref[...]
    sin = sin_ref[...]

    def norm_rope(blk, coef_a, coef_b):
        ms = jnp.mean(blk * blk, axis=0, keepdims=True)
        t = blk * lax.rsqrt(ms + EPS)
        return t * coef_a + _rope_partner(t) * coef_b

    qa_a = gq_ref[...] * cos * Q_SCALE
    qa_b = gqp_ref[...] * sin * Q_SCALE
    for j in range(N_HEADS_A):
        blk = pt[j * HEAD_DIM:(j + 1) * HEAD_DIM]
        qa_ref[0, j * HEAD_DIM:(j + 1) * HEAD_DIM, :] = norm_rope(blk, qa_a, qa_b).astype(_BF16)

    off = QA_W
    ka_a = gk_ref[...] * cos
    ka_b = gkp_ref[...] * sin
    kat = jnp.concatenate(
        [norm_rope(pt[off + j * HEAD_DIM: off + (j + 1) * HEAD_DIM], ka_a, ka_b) for j in range(N_KV_A)],
        axis=0)
    ka_ref[0] = kat.T.astype(_BF16)

    off += KV_W
    va_ref[0] = pt[off:off + KV_W].astype(_BF16)

    off += KV_W
    qb_ref[0] = (pt[off:off + QB_W] * Q_SCALE).astype(_BF16)

    off += QB_W
    kb_ref[0] = pt[off:off + KV_W].T.astype(_BF16)

    off += KV_W
    vb_ref[0] = pt[off:off + KV_W].astype(_BF16)


def _proj(x, g_pre, wt, cos_t, sin_t, gq, gqp, gk, gkp):
    bsz, s_len, _ = x.shape
    ts = PROJ_TS
    const = lambda b, i: (0, 0)
    fm = lambda b, i: (b, 0, i)
    tm = lambda b, i: (b, i, 0)
    col = pl.BlockSpec((HEAD_DIM, 1), const)
    out_shape = (
        jax.ShapeDtypeStruct((bsz, QA_W, s_len), _BF16),
        jax.ShapeDtypeStruct((bsz, s_len, KV_W), _BF16),
        jax.ShapeDtypeStruct((bsz, KV_W, s_len), _BF16),
        jax.ShapeDtypeStruct((bsz, QB_W, s_len), _BF16),
        jax.ShapeDtypeStruct((bsz, s_len, KV_W), _BF16),
        jax.ShapeDtypeStruct((bsz, KV_W, s_len), _BF16),
    )
    return pl.pallas_call(
        _proj_kernel,
        grid=(bsz, s_len // ts),
        in_specs=[
            pl.BlockSpec((1, ts, D_MODEL), tm),
            pl.BlockSpec((1, D_MODEL), const),
            pl.BlockSpec((IN_TOTAL, D_MODEL), const),
            pl.BlockSpec((HEAD_DIM, ts), lambda b, i: (0, i)),
            pl.BlockSpec((HEAD_DIM, ts), lambda b, i: (0, i)),
            col, col, col, col,
        ],
        out_specs=(
            pl.BlockSpec((1, QA_W, ts), fm),
            pl.BlockSpec((1, ts, KV_W), tm),
            pl.BlockSpec((1, KV_W, ts), fm),
            pl.BlockSpec((1, QB_W, ts), fm),
            pl.BlockSpec((1, ts, KV_W), tm),
            pl.BlockSpec((1, KV_W, ts), fm),
        ),
        out_shape=out_shape,
        compiler_params=pltpu.CompilerParams(
            dimension_semantics=("parallel", "parallel"), vmem_limit_bytes=VMEM_LIMIT_V7X),
        name="proj",
    )(x, g_pre, wt, cos_t, sin_t, gq, gqp, gk, gkp)


def _bias_kernel(table_ref, bucket_ref, out_ref):
    bucket = bucket_ref[...]
    c_idx = lax.broadcasted_iota(jnp.int32, bucket.shape, 0)
    a_idx = lax.broadcasted_iota(jnp.int32, bucket.shape, 1)
    band = jnp.abs(c_idx - BLOCK - a_idx) <= WINDOW
    edge_valid = (None, c_idx >= BLOCK, c_idx < 2 * BLOCK)
    for h in range(N_HEADS_B):
        acc = jnp.zeros(bucket.shape, _F32)
        for k in range(N_BUCKETS):
            acc = jnp.where(bucket == k, table_ref[k, h], acc)
        g, j = divmod(h, GROUP)
        for e, ok in enumerate(edge_valid):
            valid = band if ok is None else band & ok
            out_ref[g, e, :, j * BLOCK:(j + 1) * BLOCK] = jnp.where(valid, acc, NEG_INF)


def _bias(table, bucket_t):
    span = 3 * BLOCK
    shape = (N_KV_B, 3, span, GROUP * BLOCK)
    return pl.pallas_call(
        _bias_kernel,
        in_specs=[
            pl.BlockSpec(memory_space=pltpu.SMEM),
            pl.BlockSpec((span, BLOCK), lambda: (0, 0)),
        ],
        out_specs=pl.BlockSpec(shape, lambda: (0, 0, 0, 0)),
        out_shape=jax.ShapeDtypeStruct(shape, _F32),
        name="bias",
    )(table, bucket_t)


def _pad_q(q, g):
    zero = jnp.zeros_like(q)
    return jnp.concatenate([jnp.where(g == 0, q, zero), jnp.where(g == 1, q, zero)], axis=0)


def _attn_a_kernel(q_ref, k_ref, v_ref, o_ref, v_scr, s0, s1, m0, m1, *, tq, tk):
    s_len = k_ref.shape[1]
    nq = s_len // tq
    n_tiles = N_HEADS_A * nq

    ones = jnp.ones((V_EXT_ROWS - HEAD_DIM, s_len), _BF16)
    for kv in range(N_KV_A):
        v_scr[kv, 0:HEAD_DIM, :] = v_ref[0, kv * HEAD_DIM:(kv + 1) * HEAD_DIM, :]
        v_scr[kv, HEAD_DIM:, :] = ones

    def tile_slices(t):
        j = t // nq
        i = t - j * nq
        return (j // GROUP, pl.ds(pl.multiple_of(j * HEAD_DIM, HEAD_DIM), HEAD_DIM),
                pl.ds(pl.multiple_of(i * tq, tq), tq))

    n_chunks = s_len // tk

    def step(t_a, a_bufs, t_b, b_bufs):
        if t_a is not None:
            g_a, rows_a, cols_a = tile_slices(t_a)
            qp = _pad_q(q_ref[0, rows_a, cols_a], g_a)
            mx = None
        if t_b is not None:
            g_b, rows_b, cols_b = tile_slices(t_b)
            m_b = b_bufs[1][...]
            acc = jnp.zeros((V_EXT_ROWS, tq), _F32)
        for c in range(n_chunks):
            keys = slice(c * tk, (c + 1) * tk)
            if t_a is not None:
                s = _dot(k_ref[0, keys, :], qp)
                a_bufs[0][keys, :] = s
                cm = jnp.max(s.reshape(tk // 8, 8, tq), axis=0)
                mx = cm if mx is None else jnp.maximum(mx, cm)
            if t_b is not None:
                p = jnp.exp2(b_bufs[0][keys, :] - m_b).astype(_BF16)
                acc = acc + _dot(v_scr[g_b, :, keys], p)
        if t_a is not None:
            a_bufs[1][...] = jnp.max(mx, axis=0, keepdims=True)
        if t_b is not None:
            o_ref[0, rows_b, cols_b] = (acc[:HEAD_DIM] / acc[HEAD_DIM:HEAD_DIM + 1]).astype(_BF16)

    buf0, buf1 = (s0, m0), (s1, m1)
    step(0, buf0, None, None)

    def pair(u, carry):
        t = 2 * u
        step(t + 1, buf1, t, buf0)
        step(t + 2, buf0, t + 1, buf1)
        return carry

    lax.fori_loop(0, n_tiles // 2 - 1, pair, 0)
    step(n_tiles - 1, buf1, n_tiles - 2, buf0)
    step(None, None, n_tiles - 1, buf1)


def _attn_a(qt, k, vt):
    bsz, _, s_len = qt.shape
    tq = ATT_TQ
    kern = functools.partial(_attn_a_kernel, tq=tq, tk=ATT_TK)
    whole = lambda b: (b, 0, 0)
    return pl.pallas_call(
        kern,
        grid=(bsz,),
        in_specs=[
            pl.BlockSpec((1, QA_W, s_len), whole),
            pl.BlockSpec((1, s_len, KV_W), whole),
            pl.BlockSpec((1, KV_W, s_len), whole),
        ],
        out_specs=pl.BlockSpec((1, QA_W, s_len), whole),
        out_shape=jax.ShapeDtypeStruct((bsz, QA_W, s_len), _BF16),
        scratch_shapes=[
            pltpu.VMEM((N_KV_A, V_EXT_ROWS, s_len), _BF16),
            pltpu.VMEM((s_len, tq), _F32),
            pltpu.VMEM((s_len, tq), _F32),
            pltpu.VMEM((1, tq), _F32),
            pltpu.VMEM((1, tq), _F32),
        ],
        compiler_params=pltpu.CompilerParams(
            dimension_semantics=("parallel",), vmem_limit_bytes=VMEM_LIMIT_V7X),
        name="attn_a",
    )(qt, k, vt)


def _attn_b_kernel(q_ref, k_ref, v_ref, bias_ref, sink_ref, o_ref, v_scr, s0, s1, m0, m1, *, n_blocks):
    g = pl.program_id(0)
    s_len = n_blocks * BLOCK
    v_scr[0:HEAD_DIM, :] = v_ref[0]
    v_scr[HEAD_DIM:, :] = jnp.ones((V_EXT_ROWS - HEAD_DIM, s_len), _BF16)
    sink = sink_ref[0]

    def span_starts(n):
        return [pl.multiple_of(jnp.clip(n + d, 0, n_blocks - 1) * BLOCK, BLOCK) for d in (-1, 0, 1)]

    def stage_a(n, s_buf, m_buf):
        cols = pl.ds(pl.multiple_of(n * BLOCK, BLOCK), BLOCK)
        q4 = jnp.concatenate(
            [q_ref[0, j * HEAD_DIM:(j + 1) * HEAD_DIM, cols] for j in range(GROUP)], axis=1)
        k_span = jnp.concatenate([k_ref[0, pl.ds(st, BLOCK), :] for st in span_starts(n)], axis=0)
        edge = jnp.where(n == 0, 1, jnp.where(n == n_blocks - 1, 2, 0))
        s = _dot(k_span, _pad_q(q4, g)) + bias_ref[0, edge]
        s_buf[...] = s
        m_buf[...] = jnp.maximum(jnp.max(s, axis=0, keepdims=True), sink)

    def stage_b(n, s_buf, m_buf):
        cols = pl.ds(pl.multiple_of(n * BLOCK, BLOCK), BLOCK)
        m = m_buf[...]
        p = jnp.exp2(s_buf[...] - m).astype(_BF16)
        v_span = jnp.concatenate([v_scr[:, pl.ds(st, BLOCK)] for st in span_starts(n)], axis=1)
        pv = _dot(v_span, p)
        denom = pv[HEAD_DIM:HEAD_DIM + 1] + jnp.exp2(sink - m)
        o = (pv[:HEAD_DIM] / denom).astype(_BF16)
        for j in range(GROUP):
            o_ref[0, j * HEAD_DIM:(j + 1) * HEAD_DIM, cols] = o[:, j * BLOCK:(j + 1) * BLOCK]

    bufs = ((s0, m0), (s1, m1))

    def steps(n0, count):
        for i in range(count):
            stage_a(n0 + i + 1, *bufs[(i + 1) % 2])
            stage_b(n0 + i, *bufs[i % 2])

    stage_a(0, *bufs[0])
    n_loop = (n_blocks - 1) // B_UNROLL

    def body(u, carry):
        steps(u * B_UNROLL, B_UNROLL)
        return carry

    lax.fori_loop(0, n_loop, body, 0)
    done = n_loop * B_UNROLL
    steps(done, n_blocks - 1 - done)
    stage_b(n_blocks - 1, *bufs[(n_blocks - 1) % 2])


def _attn_b(qt, k, vt, bias_t, sink_row):
    bsz, _, s_len = qt.shape
    n_blocks = s_len // BLOCK
    gw = GROUP * HEAD_DIM
    kern = functools.partial(_attn_b_kernel, n_blocks=n_blocks)
    return pl.pallas_call(
        kern,
        grid=(N_KV_B, bsz),
        in_specs=[
            pl.BlockSpec((1, gw, s_len), lambda g, b: (b, g, 0)),
            pl.BlockSpec((1, s_len, KV_W), lambda g, b: (b, 0, 0)),
            pl.BlockSpec((1, HEAD_DIM, s_len), lambda g, b: (b, g, 0)),
            pl.BlockSpec((1, 3, 3 * BLOCK, GROUP * BLOCK), lambda g, b: (g, 0, 0, 0)),
            pl.BlockSpec((1, 1, GROUP * BLOCK), lambda g, b: (g, 0, 0)),
        ],
        out_specs=pl.BlockSpec((1, gw, s_len), lambda g, b: (b, g, 0)),
        out_shape=jax.ShapeDtypeStruct((bsz, QB_W, s_len), _BF16),
        scratch_shapes=[
            pltpu.VMEM((V_EXT_ROWS, s_len), _BF16),
            pltpu.VMEM((3 * BLOCK, GROUP * BLOCK), _F32),
            pltpu.VMEM((3 * BLOCK, GROUP * BLOCK), _F32),
            pltpu.VMEM((1, GROUP * BLOCK), _F32),
            pltpu.VMEM((1, GROUP * BLOCK), _F32),
        ],
        compiler_params=pltpu.CompilerParams(
            dimension_semantics=("parallel", "parallel"), vmem_limit_bytes=VMEM_LIMIT_V7X),
        name="attn_b",
    )(qt, k, vt, bias_t, sink_row)


def _post_kernel(x_ref, oa_ref, ob_ref, wo_ref, wup_ref, wdn_ref, g1_ref, g2_ref, g3_ref, out_ref):
    mix = _dot_tn(oa_ref[0], wo_ref[0:QA_W, :]) + _dot_tn(ob_ref[0], wo_ref[QA_W:, :])
    x1 = x_ref[0] + _rms_rows(mix, g1_ref[...])
    h = _rms_rows(x1, g2_ref[...]).astype(_BF16)
    f = jnp.zeros_like(x1)
    for c in range(D_FF // FF_CHUNK):
        u = jnp.maximum(_dot(h, wup_ref[:, c * FF_CHUNK:(c + 1) * FF_CHUNK]), 0.0)
        f = f + _dot((u * u).astype(_BF16), wdn_ref[c * FF_CHUNK:(c + 1) * FF_CHUNK, :])
    out_ref[0] = x1 + _rms_rows(f, g3_ref[...])


def _post(x, oa, ob, wo, wup, wdn, g1, g2, g3):
    bsz, s_len, _ = x.shape
    ts = POST_TS
    const = lambda b, i: (0, 0)
    resident = functools.partial(pl.BlockSpec, index_map=const, pipeline_mode=pl.Buffered(1))
    return pl.pallas_call(
        _post_kernel,
        grid=(bsz, s_len // ts),
        in_specs=[
            pl.BlockSpec((1, ts, D_MODEL), lambda b, i: (b, i, 0)),
            pl.BlockSpec((1, QA_W, ts), lambda b, i: (b, 0, i)),
            pl.BlockSpec((1, QB_W, ts), lambda b, i: (b, 0, i)),
            resident((D_MODEL, D_MODEL)),
            resident((D_MODEL, D_FF)),
            resident((D_FF, D_MODEL)),
            pl.BlockSpec((1, D_MODEL), const),
            pl.BlockSpec((1, D_MODEL), const),
            pl.BlockSpec((1, D_MODEL), const),
        ],
        out_specs=pl.BlockSpec((1, ts, D_MODEL), lambda b, i: (b, i, 0)),
        out_shape=jax.ShapeDtypeStruct((bsz, s_len, D_MODEL), _F32),
        compiler_params=pltpu.CompilerParams(
            dimension_semantics=("parallel", "parallel"), vmem_limit_bytes=VMEM_LIMIT_V7X),
        name="post",
    )(x, oa, ob, wo, wup, wdn, g1, g2, g3)


def _rope_tables(s_len):
    nf = HEAD_DIM // 4
    pos = jnp.arange(s_len, dtype=jnp.int32)
    row = (pos // GRID_W).astype(_F32)
    col = (pos % GRID_W).astype(_F32)
    freqs = ROPE_THETA ** (-jnp.arange(nf, dtype=_F32) / nf)
    ang_r = row[:, None] * freqs[None, :]
    ang_c = col[:, None] * freqs[None, :]
    cr, sr, cc, sc = jnp.cos(ang_r), jnp.sin(ang_r), jnp.cos(ang_c), jnp.sin(ang_c)
    cos_t = jnp.concatenate([cr, cr, cc, cc], axis=1).T
    sin_t = jnp.concatenate([-sr, sr, -sc, sc], axis=1).T
    return cos_t, sin_t


def _t5_bucket_t():
    nb = N_BUCKETS // 2
    a = jnp.arange(BLOCK, dtype=jnp.int32)
    c = jnp.arange(3 * BLOCK, dtype=jnp.int32)
    rel = c[:, None] - BLOCK - a[None, :]
    ret = (rel > 0).astype(jnp.int32) * nb
    n = jnp.abs(rel)
    max_exact = nb // 2
    nf = jnp.maximum(n, 1).astype(_F32)
    large = max_exact + (jnp.log(nf / max_exact) / np.float32(np.log(MAX_DISTANCE / max_exact))
                         * (nb - max_exact)).astype(jnp.int32)
    large = jnp.minimum(large, nb - 1)
    return ret + jnp.where(n < max_exact, n, large)


def _partner_gain(g):
    g4 = g.reshape(4, HEAD_DIM // 4)
    return jnp.stack([g4[1], g4[0], g4[3], g4[2]]).reshape(HEAD_DIM, 1)


def kernel(x, w_in, w_o, g_pre_mix, g_post_mix, q_norm_a, k_norm_a, sink_b, rel_bias,
           g_pre_ffn, w_ffn_up, w_ffn_down, g_post_ffn):
    bsz, s_len, _ = x.shape
    assert s_len % max(PROJ_TS, ATT_TQ, ATT_TK, POST_TS) == 0 and B_UNROLL % 2 == 0
    cos_t, sin_t = _rope_tables(s_len)
    bucket_t = _t5_bucket_t()
    bias_t = _bias(rel_bias * LOG2E, bucket_t)

    for l in range(w_in.shape[0]):
        gq, gk = q_norm_a[l], k_norm_a[l]
        sink_row = jnp.repeat(sink_b[l] * LOG2E, BLOCK).reshape(N_KV_B, 1, GROUP * BLOCK)
        qa, ka, va, qb, kb, vb = _proj(
            x, g_pre_mix[l][None], w_in[l].T.astype(_BF16), cos_t, sin_t,
            gq.reshape(HEAD_DIM, 1), _partner_gain(gq), gk.reshape(HEAD_DIM, 1), _partner_gain(gk))
        oa = _attn_a(qa, ka, va)
        ob = _attn_b(qb, kb, vb, bias_t, sink_row)
        x = _post(x, oa, ob, w_o[l].astype(_BF16), w_ffn_up[l].astype(_BF16), w_ffn_down[l].astype(_BF16),
                  g_post_mix[l][None], g_pre_ffn[l][None], g_post_ffn[l][None])
    return x
```

```python
import functools
import math

import numpy as np
import jax
import jax.numpy as jnp
from jax import lax
from jax.experimental import pallas as pl
from jax.experimental.pallas import tpu as pltpu

D_MODEL = 1024
HEAD_DIM = 64
N_HEADS_A = 8
N_KV_A = 2
N_HEADS_B = 8
N_KV_B = 2
GROUP = 4
D_FF = 4 * D_MODEL
GRID_W = 64
BLOCK = 128
WINDOW = 128
N_BUCKETS = 32
MAX_DISTANCE = 128
ROPE_THETA = 10000.0
EPS = 1e-6
NEG_INF = -1e30
QA_W = N_HEADS_A * HEAD_DIM
KV_W = N_KV_A * HEAD_DIM
QB_W = N_HEADS_B * HEAD_DIM
IN_TOTAL = QA_W + 2 * KV_W + QB_W + 2 * KV_W
LOG2E = math.log2(math.e)
Q_SCALE = HEAD_DIM ** -0.5 * LOG2E

VMEM_LIMIT_V7X = 56 * 1024 * 1024

PROJ_TS = 1024
ATT_TQ = 512
ATT_TK = 256
V_EXT_ROWS = HEAD_DIM + 16
B_UNROLL = 4
POST_TS = 512
FF_CHUNK = 1024
POST_SLAB = 256

_F32 = jnp.float32
_BF16 = jnp.bfloat16


def _dot(a, b):
    return jnp.dot(a, b, preferred_element_type=_F32)


def _dot_nt(a, b):
    return lax.dot_general(a, b, (((1,), (1,)), ((), ())), preferred_element_type=_F32)


def _dot_tn(a, b):
    return lax.dot_general(a, b, (((0,), (0,)), ((), ())), preferred_element_type=_F32)


def _rms_rows(x, g):
    ms = jnp.mean(x * x, axis=-1, keepdims=True)
    return x * lax.rsqrt(ms + EPS) * g


def _rope_partner(t):
    return jnp.concatenate([t[16:32], t[0:16], t[48:64], t[32:48]], axis=0)


def _proj_kernel(x_ref, g_ref, wt_ref, cos_ref, sin_ref, gq_ref, gqp_ref, gk_ref, gkp_ref,
                 qa_ref, ka_ref, va_ref, qb_ref, kb_ref, vb_ref):
    x = x_ref[0]
    h = _rms_rows(x, g_ref[...]).astype(_BF16)
    pt = _dot_nt(wt_ref[...], h)

    cos = cos_ref[...]
    sin = sin_ref[...]

    def norm_rope(blk, coef_a, coef_b):
        ms = jnp.mean(blk * blk, axis=0, keepdims=True)
        t = blk * lax.rsqrt(ms + EPS)
        return t * coef_a + _rope_partner(t) * coef_b

    qa_a = gq_ref[...] * cos * Q_SCALE
    qa_b = gqp_ref[...] * sin * Q_SCALE
    for j in range(N_HEADS_A):
        blk = pt[j * HEAD_DIM:(j + 1) * HEAD_DIM]
        qa_ref[0, j * HEAD_DIM:(j + 1) * HEAD_DIM, :] = norm_rope(blk, qa_a, qa_b).astype(_BF16)

    off = QA_W
    ka_a = gk_ref[...] * cos
    ka_b = gkp_ref[...] * sin
    kat = jnp.concatenate(
        [norm_rope(pt[off + j * HEAD_DIM: off + (j + 1) * HEAD_DIM], ka_a, ka_b) for j in range(N_KV_A)],
        axis=0)
    ka_ref[0] = kat.T.astype(_BF16)

    off += KV_W
    va_ref[0] = pt[off:off + KV_W].astype(_BF16)

    off += KV_W
    qb_ref[0] = (pt[off:off + QB_W] * Q_SCALE).astype(_BF16)

    off += QB_W
    kb_ref[0] = pt[off:off + KV_W].T.astype(_BF16)

    off += KV_W
    vb_ref[0] = pt[off:off + KV_W].astype(_BF16)


def _proj(x, g_pre, wt, cos_t, sin_t, gq, gqp, gk, gkp):
    bsz, s_len, _ = x.shape
    ts = PROJ_TS
    const = lambda b, i: (0, 0)
    fm = lambda b, i: (b, 0, i)
    tm = lambda b, i: (b, i, 0)
    col = pl.BlockSpec((HEAD_DIM, 1), const)
    out_shape = (
        jax.ShapeDtypeStruct((bsz, QA_W, s_len), _BF16),
        jax.ShapeDtypeStruct((bsz, s_len, KV_W), _BF16),
        jax.ShapeDtypeStruct((bsz, KV_W, s_len), _BF16),
        jax.ShapeDtypeStruct((bsz, QB_W, s_len), _BF16),
        jax.ShapeDtypeStruct((bsz, s_len, KV_W), _BF16),
        jax.ShapeDtypeStruct((bsz, KV_W, s_len), _BF16),
    )
    return pl.pallas_call(
        _proj_kernel,
        grid=(bsz, s_len // ts),
        in_specs=[
            pl.BlockSpec((1, ts, D_MODEL), tm),
            pl.BlockSpec((1, D_MODEL), const),
            pl.BlockSpec((IN_TOTAL, D_MODEL), const),
            pl.BlockSpec((HEAD_DIM, ts), lambda b, i: (0, i)),
            pl.BlockSpec((HEAD_DIM, ts), lambda b, i: (0, i)),
            col, col, col, col,
        ],
        out_specs=(
            pl.BlockSpec((1, QA_W, ts), fm),
            pl.BlockSpec((1, ts, KV_W), tm),
            pl.BlockSpec((1, KV_W, ts), fm),
            pl.BlockSpec((1, QB_W, ts), fm),
            pl.BlockSpec((1, ts, KV_W), tm),
            pl.BlockSpec((1, KV_W, ts), fm),
        ),
        out_shape=out_shape,
        compiler_params=pltpu.CompilerParams(
            dimension_semantics=("parallel", "parallel"), vmem_limit_bytes=VMEM_LIMIT_V7X),
        name="proj",
    )(x, g_pre, wt, cos_t, sin_t, gq, gqp, gk, gkp)


def _bias_kernel(table_ref, bucket_ref, out_ref):
    bucket = bucket_ref[...]
    c_idx = lax.broadcasted_iota(jnp.int32, bucket.shape, 0)
    a_idx = lax.broadcasted_iota(jnp.int32, bucket.shape, 1)
    band = jnp.abs(c_idx - BLOCK - a_idx) <= WINDOW
    edge_valid = (None, c_idx >= BLOCK, c_idx < 2 * BLOCK)
    for h in range(N_HEADS_B):
        acc = jnp.zeros(bucket.shape, _F32)
        for k in range(N_BUCKETS):
            acc = jnp.where(bucket == k, table_ref[k, h], acc)
        g, j = divmod(h, GROUP)
        for e, ok in enumerate(edge_valid):
            valid = band if ok is None else band & ok
            out_ref[g, e, :, j * BLOCK:(j + 1) * BLOCK] = jnp.where(valid, acc, NEG_INF)


def _bias(table, bucket_t):
    span = 3 * BLOCK
    shape = (N_KV_B, 3, span, GROUP * BLOCK)
    return pl.pallas_call(
        _bias_kernel,
        in_specs=[
            pl.BlockSpec(memory_space=pltpu.SMEM),
            pl.BlockSpec((span, BLOCK), lambda: (0, 0)),
        ],
        out_specs=pl.BlockSpec(shape, lambda: (0, 0, 0, 0)),
        out_shape=jax.ShapeDtypeStruct(shape, _F32),
        name="bias",
    )(table, bucket_t)


def _pad_q(q, g):
    zero = jnp.zeros_like(q)
    return jnp.concatenate([jnp.where(g == 0, q, zero), jnp.where(g == 1, q, zero)], axis=0)


def _attn_a_kernel(q_ref, k_ref, v_ref, o_ref, v_scr, s0, s1, m0, m1, *, tq, tk):
    s_len = k_ref.shape[1]
    nq = s_len // tq
    n_tiles = N_HEADS_A * nq

    ones = jnp.ones((V_EXT_ROWS - HEAD_DIM, s_len), _BF16)
    for kv in range(N_KV_A):
        v_scr[kv, 0:HEAD_DIM, :] = v_ref[0, kv * HEAD_DIM:(kv + 1) * HEAD_DIM, :]
        v_scr[kv, HEAD_DIM:, :] = ones

    def tile_slices(t):
        j = t // nq
        i = t - j * nq
        return (j // GROUP, pl.ds(pl.multiple_of(j * HEAD_DIM, HEAD_DIM), HEAD_DIM),
                pl.ds(pl.multiple_of(i * tq, tq), tq))

    n_chunks = s_len // tk

    def step(t_a, a_bufs, t_b, b_bufs):
        if t_a is not None:
            g_a, rows_a, cols_a = tile_slices(t_a)
            qp = _pad_q(q_ref[0, rows_a, cols_a], g_a)
            mx = None
        if t_b is not None:
            g_b, rows_b, cols_b = tile_slices(t_b)
            m_b = b_bufs[1][...]
            acc = jnp.zeros((V_EXT_ROWS, tq), _F32)
        for c in range(n_chunks):
            keys = slice(c * tk, (c + 1) * tk)
            if t_a is not None:
                s = _dot(k_ref[0, keys, :], qp)
                a_bufs[0][keys, :] = s
                cm = jnp.max(s.reshape(tk // 8, 8, tq), axis=0)
                mx = cm if mx is None else jnp.maximum(mx, cm)
            if t_b is not None:
                p = jnp.exp2(b_bufs[0][keys, :] - m_b).astype(_BF16)
                acc = acc + _dot(v_scr[g_b, :, keys], p)
        if t_a is not None:
            a_bufs[1][...] = jnp.max(mx, axis=0, keepdims=True)
        if t_b is not None:
            o_ref[0, rows_b, cols_b] = (acc[:HEAD_DIM] / acc[HEAD_DIM:HEAD_DIM + 1]).astype(_BF16)

    buf0, buf1 = (s0, m0), (s1, m1)
    step(0, buf0, None, None)

    def pair(u, carry):
        t = 2 * u
        step(t + 1, buf1, t, buf0)
        step(t + 2, buf0, t + 1, buf1)
        return carry

    lax.fori_loop(0, n_tiles // 2 - 1, pair, 0)
    step(n_tiles - 1, buf1, n_tiles - 2, buf0)
    step(None, None, n_tiles - 1, buf1)


def _attn_a(qt, k, vt):
    bsz, _, s_len = qt.shape
    tq = ATT_TQ
    kern = functools.partial(_attn_a_kernel, tq=tq, tk=ATT_TK)
    whole = lambda b: (b, 0, 0)
    return pl.pallas_call(
        kern,
        grid=(bsz,),
        in_specs=[
            pl.BlockSpec((1, QA_W, s_len), whole),
            pl.BlockSpec((1, s_len, KV_W), whole),
            pl.BlockSpec((1, KV_W, s_len), whole),
        ],
        out_specs=pl.BlockSpec((1, QA_W, s_len), whole),
        out_shape=jax.ShapeDtypeStruct((bsz, QA_W, s_len), _BF16),
        scratch_shapes=[
            pltpu.VMEM((N_KV_A, V_EXT_ROWS, s_len), _BF16),
            pltpu.VMEM((s_len, tq), _F32),
            pltpu.VMEM((s_len, tq), _F32),
            pltpu.VMEM((1, tq), _F32),
            pltpu.VMEM((1, tq), _F32),
        ],
        compiler_params=pltpu.CompilerParams(
            dimension_semantics=("parallel",), vmem_limit_bytes=VMEM_LIMIT_V7X),
        name="attn_a",
    )(qt, k, vt)


def _attn_b_kernel(q_ref, k_ref, v_ref, bias_ref, sink_ref, o_ref, v_scr, s0, s1, m0, m1, *, n_blocks):
    g = pl.program_id(0)
    s_len = n_blocks * BLOCK
    v_scr[0:HEAD_DIM, :] = v_ref[0]
    v_scr[HEAD_DIM:, :] = jnp.ones((V_EXT_ROWS - HEAD_DIM, s_len), _BF16)
    sink = sink_ref[0]

    def span_starts(n):
        return [pl.multiple_of(jnp.clip(n + d, 0, n_blocks - 1) * BLOCK, BLOCK) for d in (-1, 0, 1)]

    def stage_a(n, s_buf, m_buf):
        cols = pl.ds(pl.multiple_of(n * BLOCK, BLOCK), BLOCK)
        q4 = jnp.concatenate(
            [q_ref[0, j * HEAD_DIM:(j + 1) * HEAD_DIM, cols] for j in range(GROUP)], axis=1)
        k_span = jnp.concatenate([k_ref[0, pl.ds(st, BLOCK), :] for st in span_starts(n)], axis=0)
        edge = jnp.where(n == 0, 1, jnp.where(n == n_blocks - 1, 2, 0))
        s = _dot(k_span, _pad_q(q4, g)) + bias_ref[0, edge]
        s_buf[...] = s
        m_buf[...] = jnp.maximum(jnp.max(s, axis=0, keepdims=True), sink)

    def stage_b(n, s_buf, m_buf):
        cols = pl.ds(pl.multiple_of(n * BLOCK, BLOCK), BLOCK)
        m = m_buf[...]
        p = jnp.exp2(s_buf[...] - m).astype(_BF16)
        v_span = jnp.concatenate([v_scr[:, pl.ds(st, BLOCK)] for st in span_starts(n)], axis=1)
        pv = _dot(v_span, p)
        denom = pv[HEAD_DIM:HEAD_DIM + 1] + jnp.exp2(sink - m)
        o = (pv[:HEAD_DIM] / denom).astype(_BF16)
        for j in range(GROUP):
            o_ref[0, j * HEAD_DIM:(j + 1) * HEAD_DIM, cols] = o[:, j * BLOCK:(j + 1) * BLOCK]

    bufs = ((s0, m0), (s1, m1))

    def steps(n0, count):
        for i in range(count):
            stage_a(n0 + i + 1, *bufs[(i + 1) % 2])
            stage_b(n0 + i, *bufs[i % 2])

    stage_a(0, *bufs[0])
    n_loop = (n_blocks - 1) // B_UNROLL

    def body(u, carry):
        steps(u * B_UNROLL, B_UNROLL)
        return carry

    lax.fori_loop(0, n_loop, body, 0)
    done = n_loop * B_UNROLL
    steps(done, n_blocks - 1 - done)
    stage_b(n_blocks - 1, *bufs[(n_blocks - 1) % 2])


def _attn_b(qt, k, vt, bias_t, sink_row):
    bsz, _, s_len = qt.shape
    n_blocks = s_len // BLOCK
    gw = GROUP * HEAD_DIM
    kern = functools.partial(_attn_b_kernel, n_blocks=n_blocks)
    return pl.pallas_call(
        kern,
        grid=(N_KV_B, bsz),
        in_specs=[
            pl.BlockSpec((1, gw, s_len), lambda g, b: (b, g, 0)),
            pl.BlockSpec((1, s_len, KV_W), lambda g, b: (b, 0, 0)),
            pl.BlockSpec((1, HEAD_DIM, s_len), lambda g, b: (b, g, 0)),
            pl.BlockSpec((1, 3, 3 * BLOCK, GROUP * BLOCK), lambda g, b: (g, 0, 0, 0)),
            pl.BlockSpec((1, 1, GROUP * BLOCK), lambda g, b: (g, 0, 0)),
        ],
        out_specs=pl.BlockSpec((1, gw, s_len), lambda g, b: (b, g, 0)),
        out_shape=jax.ShapeDtypeStruct((bsz, QB_W, s_len), _BF16),
        scratch_shapes=[
            pltpu.VMEM((V_EXT_ROWS, s_len), _BF16),
            pltpu.VMEM((3 * BLOCK, GROUP * BLOCK), _F32),
            pltpu.VMEM((3 * BLOCK, GROUP * BLOCK), _F32),
            pltpu.VMEM((1, GROUP * BLOCK), _F32),
            pltpu.VMEM((1, GROUP * BLOCK), _F32),
        ],
        compiler_params=pltpu.CompilerParams(
            dimension_semantics=("parallel", "parallel"), vmem_limit_bytes=VMEM_LIMIT_V7X),
        name="attn_b",
    )(qt, k, vt, bias_t, sink_row)


def _post_kernel(x_f, oa_f, ob_f, x_b, oa_b, ob_b, x_n, oa_n, ob_n, wo_ref, wup_ref, wdn_ref,
                 g1_ref, g2_ref, g3_ref, out_ref, x1_s, h_s):
    n = pl.program_id(0)
    n_chunks = D_FF // FF_CHUNK
    first, second = range(0, n_chunks // 2), range(n_chunks // 2, n_chunks)

    def mix_norm(x_ref, oa_ref, ob_ref):
        mix = _dot_tn(oa_ref[0], wo_ref[0:QA_W, :]) + _dot_tn(ob_ref[0], wo_ref[QA_W:, :])
        x1 = x_ref[0] + _rms_rows(mix, g1_ref[...])
        return x1, _rms_rows(x1, g2_ref[...]).astype(_BF16)

    def mlp_chunks(h, f, chunks):
        for c in chunks:
            u = jnp.maximum(_dot(h, wup_ref[:, c * FF_CHUNK:(c + 1) * FF_CHUNK]), 0.0)
            fc = _dot((u * u).astype(_BF16), wdn_ref[c * FF_CHUNK:(c + 1) * FF_CHUNK, :])
            f = fc if f is None else f + fc
        return f

    @pl.when(n == 0)
    def _():
        x1_s[...], h_s[...] = mix_norm(x_f, oa_f, ob_f)

    h_a = h_s[...]
    f = mlp_chunks(h_a, None, first)
    x1_b, h_b = mix_norm(x_b, oa_b, ob_b)
    f = mlp_chunks(h_a, f, second)
    out_ref[0, 0:POST_SLAB, :] = x1_s[...] + _rms_rows(f, g3_ref[...])

    f = mlp_chunks(h_b, None, first)
    x1_s[...], h_s[...] = mix_norm(x_n, oa_n, ob_n)
    f = mlp_chunks(h_b, f, second)
    out_ref[0, POST_SLAB:, :] = x1_b + _rms_rows(f, g3_ref[...])


def _post(x, oa, ob, wo, wup, wdn, g1, g2, g3):
    bsz, s_len, _ = x.shape
    ts = POST_TS
    assert ts == 2 * POST_SLAB
    tps = s_len // ts
    n_tiles = bsz * tps
    const = lambda n: (0, 0)
    resident = functools.partial(pl.BlockSpec, index_map=const, pipeline_mode=pl.Buffered(1))

    def slab(t, half, feature_major):
        b, i = t // tps, 2 * (t % tps) + half
        return (b, 0, i) if feature_major else (b, i, 0)

    nxt = lambda n: jnp.minimum(n + 1, n_tiles - 1)
    x_spec = lambda f, **kw: pl.BlockSpec((1, POST_SLAB, D_MODEL), f, **kw)
    oa_spec = lambda f, **kw: pl.BlockSpec((1, QA_W, POST_SLAB), f, **kw)
    ob_spec = lambda f, **kw: pl.BlockSpec((1, QB_W, POST_SLAB), f, **kw)
    once = dict(pipeline_mode=pl.Buffered(1))
    return pl.pallas_call(
        _post_kernel,
        grid=(n_tiles,),
        in_specs=[
            x_spec(lambda n: (0, 0, 0), **once),
            oa_spec(lambda n: (0, 0, 0), **once),
            ob_spec(lambda n: (0, 0, 0), **once),
            x_spec(lambda n: slab(n, 1, False)),
            oa_spec(lambda n: slab(n, 1, True)),
            ob_spec(lambda n: slab(n, 1, True)),
            x_spec(lambda n: slab(nxt(n), 0, False)),
            oa_spec(lambda n: slab(nxt(n), 0, True)),
            ob_spec(lambda n: slab(nxt(n), 0, True)),
            resident((D_MODEL, D_MODEL)),
            resident((D_MODEL, D_FF)),
            resident((D_FF, D_MODEL)),
            pl.BlockSpec((1, D_MODEL), const),
            pl.BlockSpec((1, D_MODEL), const),
            pl.BlockSpec((1, D_MODEL), const),
        ],
        out_specs=pl.BlockSpec((1, ts, D_MODEL), lambda n: (n // tps, n % tps, 0)),
        out_shape=jax.ShapeDtypeStruct((bsz, s_len, D_MODEL), _F32),
        scratch_shapes=[pltpu.VMEM((POST_SLAB, D_MODEL), _F32), pltpu.VMEM((POST_SLAB, D_MODEL), _BF16)],
        compiler_params=pltpu.CompilerParams(
            dimension_semantics=("arbitrary",), vmem_limit_bytes=VMEM_LIMIT_V7X),
        name="post",
    )(x, oa, ob, x, oa, ob, x, oa, ob, wo, wup, wdn, g1, g2, g3)


def _rope_tables(s_len):
    nf = HEAD_DIM // 4
    pos = jnp.arange(s_len, dtype=jnp.int32)
    row = (pos // GRID_W).astype(_F32)
    col = (pos % GRID_W).astype(_F32)
    freqs = ROPE_THETA ** (-jnp.arange(nf, dtype=_F32) / nf)
    ang_r = row[:, None] * freqs[None, :]
    ang_c = col[:, None] * freqs[None, :]
    cr, sr, cc, sc = jnp.cos(ang_r), jnp.sin(ang_r), jnp.cos(ang_c), jnp.sin(ang_c)
    cos_t = jnp.concatenate([cr, cr, cc, cc], axis=1).T
    sin_t = jnp.concatenate([-sr, sr, -sc, sc], axis=1).T
    return cos_t, sin_t


def _t5_bucket_t():
    nb = N_BUCKETS // 2
    a = jnp.arange(BLOCK, dtype=jnp.int32)
    c = jnp.arange(3 * BLOCK, dtype=jnp.int32)
    rel = c[:, None] - BLOCK - a[None, :]
    ret = (rel > 0).astype(jnp.int32) * nb
    n = jnp.abs(rel)
    max_exact = nb // 2
    nf = jnp.maximum(n, 1).astype(_F32)
    large = max_exact + (jnp.log(nf / max_exact) / np.float32(np.log(MAX_DISTANCE / max_exact))
                         * (nb - max_exact)).astype(jnp.int32)
    large = jnp.minimum(large, nb - 1)
    return ret + jnp.where(n < max_exact, n, large)


def _partner_gain(g):
    g4 = g.reshape(4, HEAD_DIM // 4)
    return jnp.stack([g4[1], g4[0], g4[3], g4[2]]).reshape(HEAD_DIM, 1)


def kernel(x, w_in, w_o, g_pre_mix, g_post_mix, q_norm_a, k_norm_a, sink_b, rel_bias,
           g_pre_ffn, w_ffn_up, w_ffn_down, g_post_ffn):
    bsz, s_len, _ = x.shape
    assert s_len % max(PROJ_TS, ATT_TQ, ATT_TK, POST_TS) == 0 and B_UNROLL % 2 == 0
    cos_t, sin_t = _rope_tables(s_len)
    bucket_t = _t5_bucket_t()
    bias_t = _bias(rel_bias * LOG2E, bucket_t)

    for l in range(w_in.shape[0]):
        gq, gk = q_norm_a[l], k_norm_a[l]
        sink_row = jnp.repeat(sink_b[l] * LOG2E, BLOCK).reshape(N_KV_B, 1, GROUP * BLOCK)
        qa, ka, va, qb, kb, vb = _proj(
            x, g_pre_mix[l][None], w_in[l].T.astype(_BF16), cos_t, sin_t,
            gq.reshape(HEAD_DIM, 1), _partner_gain(gq), gk.reshape(HEAD_DIM, 1), _partner_gain(gk))
        oa = _attn_a(qa, ka, va)
        ob = _attn_b(qb, kb, vb, bias_t, sink_row)
        x = _post(x, oa, ob, w_o[l].astype(_BF16), w_ffn_up[l].astype(_BF16), w_ffn_down[l].astype(_BF16),
                  g_post_mix[l][None], g_pre_ffn[l][None], g_post_ffn[l][None])
    return x
```

```python
import functools
import math

import numpy as np
import jax
import jax.numpy as jnp
from jax import lax
from jax.experimental import pallas as pl
from jax.experimental.pallas import tpu as pltpu

D_MODEL = 1024
HEAD_DIM = 64
N_HEADS_A = 8
N_KV_A = 2
N_HEADS_B = 8
N_KV_B = 2
GROUP = 4
D_FF = 4 * D_MODEL
GRID_W = 64
BLOCK = 128
WINDOW = 128
N_BUCKETS = 32
MAX_DISTANCE = 128
ROPE_THETA = 10000.0
EPS = 1e-6
NEG_INF = -1e30
QA_W = N_HEADS_A * HEAD_DIM
KV_W = N_KV_A * HEAD_DIM
QB_W = N_HEADS_B * HEAD_DIM
IN_TOTAL = QA_W + 2 * KV_W + QB_W + 2 * KV_W
LOG2E = math.log2(math.e)
Q_SCALE = HEAD_DIM ** -0.5 * LOG2E

VMEM_LIMIT_V7X = 56 * 1024 * 1024

PROJ_TS = 2048
PROJ_SLAB = 1024
PROJ_PIECE = 256
ATT_TQ = 512
ATT_TK = 256
B_LAG = 1
V_EXT_ROWS = HEAD_DIM + 16
B_BUFS = 4
B_UNROLL = 12
POST_TS = 1024
POST_SLAB = 256
FF_CHUNK = 1024

_F32 = jnp.float32
_BF16 = jnp.bfloat16


def _dot(a, b):
    return jnp.dot(a, b, preferred_element_type=_F32)


def _dot_nt(a, b):
    return lax.dot_general(a, b, (((1,), (1,)), ((), ())), preferred_element_type=_F32)


def _dot_tn(a, b):
    return lax.dot_general(a, b, (((0,), (0,)), ((), ())), preferred_element_type=_F32)


def _rms_rows(x, g):
    ms = jnp.mean(x * x, axis=-1, keepdims=True)
    return x * lax.rsqrt(ms + EPS) * g


def _rope_partner(t):
    return jnp.concatenate([t[16:32], t[0:16], t[48:64], t[32:48]], axis=0)


def _proj_kernel(x_ref, g_ref, wt_ref, cos_ref, sin_ref, gq_ref, gqp_ref, gk_ref, gkp_ref,
                 qa_ref, ka_ref, va_ref, qb_ref, kb_ref, vb_ref):
    def norm_rope(blk, coef_a, coef_b):
        ms = jnp.mean(blk * blk, axis=0, keepdims=True)
        t = blk * lax.rsqrt(ms + EPS)
        return t * coef_a + _rope_partner(t) * coef_b

    ts = x_ref.shape[1]
    for t0 in range(0, ts, PROJ_SLAB):
        tok = slice(t0, t0 + PROJ_SLAB)
        h = _rms_rows(x_ref[0, tok, :], g_ref[...]).astype(_BF16)

        def piece(lo, rows):
            return _dot_nt(wt_ref[lo:lo + rows, :], h)

        cos = cos_ref[:, tok]
        sin = sin_ref[:, tok]
        qa_a = gq_ref[...] * cos * Q_SCALE
        qa_b = gqp_ref[...] * sin * Q_SCALE
        for lo in range(0, QA_W, PROJ_PIECE):
            pt = piece(lo, PROJ_PIECE)
            for j in range(PROJ_PIECE // HEAD_DIM):
                blk = pt[j * HEAD_DIM:(j + 1) * HEAD_DIM]
                r = lo + j * HEAD_DIM
                qa_ref[0, r:r + HEAD_DIM, tok] = norm_rope(blk, qa_a, qa_b).astype(_BF16)

        off = QA_W
        pt = piece(off, 2 * KV_W)
        ka_a = gk_ref[...] * cos
        ka_b = gkp_ref[...] * sin
        kat = jnp.concatenate(
            [norm_rope(pt[j * HEAD_DIM:(j + 1) * HEAD_DIM], ka_a, ka_b) for j in range(N_KV_A)],
            axis=0)
        ka_ref[0, tok, :] = kat.T.astype(_BF16)
        va_ref[0, :, tok] = pt[KV_W:].astype(_BF16)

        off += 2 * KV_W
        for lo in range(0, QB_W, PROJ_PIECE):
            qb_ref[0, lo:lo + PROJ_PIECE, tok] = (piece(off + lo, PROJ_PIECE) * Q_SCALE).astype(_BF16)

        off += QB_W
        pt = piece(off, 2 * KV_W)
        kb_ref[0, tok, :] = pt[:KV_W].T.astype(_BF16)
        vb_ref[0, :, tok] = pt[KV_W:].astype(_BF16)


def _proj(x, g_pre, wt, cos_t, sin_t, gq, gqp, gk, gkp):
    bsz, s_len, _ = x.shape
    ts = PROJ_TS
    const = lambda b, i: (0, 0)
    fm = lambda b, i: (b, 0, i)
    tm = lambda b, i: (b, i, 0)
    col = pl.BlockSpec((HEAD_DIM, 1), const)
    out_shape = (
        jax.ShapeDtypeStruct((bsz, QA_W, s_len), _BF16),
        jax.ShapeDtypeStruct((bsz, s_len, KV_W), _BF16),
        jax.ShapeDtypeStruct((bsz, KV_W, s_len), _BF16),
        jax.ShapeDtypeStruct((bsz, QB_W, s_len), _BF16),
        jax.ShapeDtypeStruct((bsz, s_len, KV_W), _BF16),
        jax.ShapeDtypeStruct((bsz, KV_W, s_len), _BF16),
    )
    return pl.pallas_call(
        _proj_kernel,
        grid=(bsz, s_len // ts),
        in_specs=[
            pl.BlockSpec((1, ts, D_MODEL), tm),
            pl.BlockSpec((1, D_MODEL), const),
            pl.BlockSpec((IN_TOTAL, D_MODEL), const),
            pl.BlockSpec((HEAD_DIM, ts), lambda b, i: (0, i)),
            pl.BlockSpec((HEAD_DIM, ts), lambda b, i: (0, i)),
            col, col, col, col,
        ],
        out_specs=(
            pl.BlockSpec((1, QA_W, ts), fm),
            pl.BlockSpec((1, ts, KV_W), tm),
            pl.BlockSpec((1, KV_W, ts), fm),
            pl.BlockSpec((1, QB_W, ts), fm),
            pl.BlockSpec((1, ts, KV_W), tm),
            pl.BlockSpec((1, KV_W, ts), fm),
        ),
        out_shape=out_shape,
        compiler_params=pltpu.CompilerParams(
            dimension_semantics=("parallel", "parallel"), vmem_limit_bytes=VMEM_LIMIT_V7X),
        name="proj",
    )(x, g_pre, wt, cos_t, sin_t, gq, gqp, gk, gkp)


def _bias_kernel(table_ref, bucket_ref, out_ref):
    bucket = bucket_ref[...]
    c_idx = lax.broadcasted_iota(jnp.int32, bucket.shape, 0)
    a_idx = lax.broadcasted_iota(jnp.int32, bucket.shape, 1)
    band = jnp.abs(c_idx - BLOCK - a_idx) <= WINDOW
    edge_valid = (None, c_idx >= BLOCK, c_idx < 2 * BLOCK)
    for h in range(N_HEADS_B):
        acc = jnp.zeros(bucket.shape, _F32)
        for k in range(N_BUCKETS):
            acc = jnp.where(bucket == k, table_ref[k, h], acc)
        g, j = divmod(h, GROUP)
        for e, ok in enumerate(edge_valid):
            valid = band if ok is None else band & ok
            out_ref[g, e, :, j * BLOCK:(j + 1) * BLOCK] = jnp.where(valid, acc, NEG_INF)


def _bias(table, bucket_t):
    span = 3 * BLOCK
    shape = (N_KV_B, 3, span, GROUP * BLOCK)
    return pl.pallas_call(
        _bias_kernel,
        in_specs=[
            pl.BlockSpec(memory_space=pltpu.SMEM),
            pl.BlockSpec((span, BLOCK), lambda: (0, 0)),
        ],
        out_specs=pl.BlockSpec(shape, lambda: (0, 0, 0, 0)),
        out_shape=jax.ShapeDtypeStruct(shape, _F32),
        name="bias",
    )(table, bucket_t)


def _pad_q(q, g):
    zero = jnp.zeros_like(q)
    return jnp.concatenate([jnp.where(g == 0, q, zero), jnp.where(g == 1, q, zero)], axis=0)


def _attn_a_kernel(q_ref, k_ref, v_ref, o_ref, v_scr, s0, s1, m0, m1, *, tq, tk):
    s_len = k_ref.shape[1]
    nq = s_len // tq
    n_tiles = N_HEADS_A * nq

    ones = jnp.ones((V_EXT_ROWS - HEAD_DIM, s_len), _BF16)
    for kv in range(N_KV_A):
        v_scr[kv, 0:HEAD_DIM, :] = v_ref[0, kv * HEAD_DIM:(kv + 1) * HEAD_DIM, :]
        v_scr[kv, HEAD_DIM:, :] = ones

    def tile_slices(t):
        j = t // nq
        i = t - j * nq
        return (j // GROUP, pl.ds(pl.multiple_of(j * HEAD_DIM, HEAD_DIM), HEAD_DIM),
                pl.ds(pl.multiple_of(i * tq, tq), tq))

    n_chunks = s_len // tk

    def step(t_a, a_bufs, t_b, b_bufs):
        if t_a is not None:
            g_a, rows_a, cols_a = tile_slices(t_a)
            qp = _pad_q(q_ref[0, rows_a, cols_a], g_a)
            mx = None
        if t_b is not None:
            g_b, rows_b, cols_b = tile_slices(t_b)
            m_b = b_bufs[1][...]
            acc = jnp.zeros((V_EXT_ROWS, tq), _F32)
        for c in range(n_chunks + B_LAG):
            if t_a is not None and c < n_chunks:
                keys = slice(c * tk, (c + 1) * tk)
                s = _dot(k_ref[0, keys, :], qp)
                a_bufs[0][keys, :] = s
                cm = jnp.max(s.reshape(tk // 8, 8, tq), axis=0)
                mx = cm if mx is None else jnp.maximum(mx, cm)
            if t_b is not None and c >= B_LAG:
                keys = slice((c - B_LAG) * tk, (c - B_LAG + 1) * tk)
                p = jnp.exp2(b_bufs[0][keys, :] - m_b).astype(_BF16)
                acc = acc + _dot(v_scr[g_b, :, keys], p)
        if t_a is not None:
            a_bufs[1][...] = jnp.max(mx, axis=0, keepdims=True)
        if t_b is not None:
            o_ref[0, rows_b, cols_b] = (acc[:HEAD_DIM] / acc[HEAD_DIM:HEAD_DIM + 1]).astype(_BF16)

    buf0, buf1 = (s0, m0), (s1, m1)
    step(0, buf0, None, None)

    def pair(u, carry):
        t = 2 * u
        step(t + 1, buf1, t, buf0)
        step(t + 2, buf0, t + 1, buf1)
        return carry

    lax.fori_loop(0, n_tiles // 2 - 1, pair, 0)
    step(n_tiles - 1, buf1, n_tiles - 2, buf0)
    step(None, None, n_tiles - 1, buf1)


def _attn_a(qt, k, vt):
    bsz, _, s_len = qt.shape
    tq = ATT_TQ
    kern = functools.partial(_attn_a_kernel, tq=tq, tk=ATT_TK)
    whole = lambda b: (b, 0, 0)
    return pl.pallas_call(
        kern,
        grid=(bsz,),
        in_specs=[
            pl.BlockSpec((1, QA_W, s_len), whole),
            pl.BlockSpec((1, s_len, KV_W), whole),
            pl.BlockSpec((1, KV_W, s_len), whole),
        ],
        out_specs=pl.BlockSpec((1, QA_W, s_len), whole),
        out_shape=jax.ShapeDtypeStruct((bsz, QA_W, s_len), _BF16),
        scratch_shapes=[
            pltpu.VMEM((N_KV_A, V_EXT_ROWS, s_len), _BF16),
            pltpu.VMEM((s_len, tq), _F32),
            pltpu.VMEM((s_len, tq), _F32),
            pltpu.VMEM((1, tq), _F32),
            pltpu.VMEM((1, tq), _F32),
        ],
        compiler_params=pltpu.CompilerParams(
            dimension_semantics=("parallel",), vmem_limit_bytes=VMEM_LIMIT_V7X),
        name="attn_a",
    )(qt, k, vt)


def _attn_b_kernel(q_ref, k_ref, v_ref, bias_ref, sink_ref, o_ref, v_scr, *sm, n_blocks):
    g = pl.program_id(0)
    s_len = n_blocks * BLOCK
    v_scr[0:HEAD_DIM, :] = v_ref[0]
    v_scr[HEAD_DIM:, :] = jnp.ones((V_EXT_ROWS - HEAD_DIM, s_len), _BF16)
    sink = sink_ref[0]

    def span_starts(n):
        return [pl.multiple_of(jnp.clip(n + d, 0, n_blocks - 1) * BLOCK, BLOCK) for d in (-1, 0, 1)]

    def stage_a(n, s_buf, m_buf):
        cols = pl.ds(pl.multiple_of(n * BLOCK, BLOCK), BLOCK)
        q4 = jnp.concatenate(
            [q_ref[0, j * HEAD_DIM:(j + 1) * HEAD_DIM, cols] for j in range(GROUP)], axis=1)
        k_span = jnp.concatenate([k_ref[0, pl.ds(st, BLOCK), :] for st in span_starts(n)], axis=0)
        edge = jnp.where(n == 0, 1, jnp.where(n == n_blocks - 1, 2, 0))
        s = _dot(k_span, _pad_q(q4, g)) + bias_ref[0, edge]
        s_buf[...] = s
        m_buf[...] = jnp.maximum(jnp.max(s, axis=0, keepdims=True), sink)

    def stage_b(n, s_buf, m_buf):
        cols = pl.ds(pl.multiple_of(n * BLOCK, BLOCK), BLOCK)
        m = m_buf[...]
        p = jnp.exp2(s_buf[...] - m).astype(_BF16)
        v_span = jnp.concatenate([v_scr[:, pl.ds(st, BLOCK)] for st in span_starts(n)], axis=1)
        pv = _dot(v_span, p)
        denom = pv[HEAD_DIM:HEAD_DIM + 1] + jnp.exp2(sink - m)
        o = (pv[:HEAD_DIM] / denom).astype(_BF16)
        for j in range(GROUP):
            o_ref[0, j * HEAD_DIM:(j + 1) * HEAD_DIM, cols] = o[:, j * BLOCK:(j + 1) * BLOCK]

    bufs = tuple(zip(sm[:B_BUFS], sm[B_BUFS:]))
    ahead = B_BUFS - 1

    def steps(n0, count):
        for i in range(count):
            stage_a(n0 + i + ahead, *bufs[(i + ahead) % B_BUFS])
            stage_b(n0 + i, *bufs[i % B_BUFS])

    for n in range(ahead):
        stage_a(n, *bufs[n])
    n_loop = (n_blocks - ahead) // B_UNROLL

    def body(u, carry):
        steps(u * B_UNROLL, B_UNROLL)
        return carry

    lax.fori_loop(0, n_loop, body, 0)
    done = n_loop * B_UNROLL
    steps(done, n_blocks - ahead - done)
    for n in range(n_blocks - ahead, n_blocks):
        stage_b(n, *bufs[n % B_BUFS])


def _attn_b(qt, k, vt, bias_t, sink_row):
    bsz, _, s_len = qt.shape
    n_blocks = s_len // BLOCK
    gw = GROUP * HEAD_DIM
    kern = functools.partial(_attn_b_kernel, n_blocks=n_blocks)
    return pl.pallas_call(
        kern,
        grid=(N_KV_B, bsz),
        in_specs=[
            pl.BlockSpec((1, gw, s_len), lambda g, b: (b, g, 0)),
            pl.BlockSpec((1, s_len, KV_W), lambda g, b: (b, 0, 0)),
            pl.BlockSpec((1, HEAD_DIM, s_len), lambda g, b: (b, g, 0)),
            pl.BlockSpec((1, 3, 3 * BLOCK, GROUP * BLOCK), lambda g, b: (g, 0, 0, 0)),
            pl.BlockSpec((1, 1, GROUP * BLOCK), lambda g, b: (g, 0, 0)),
        ],
        out_specs=pl.BlockSpec((1, gw, s_len), lambda g, b: (b, g, 0)),
        out_shape=jax.ShapeDtypeStruct((bsz, QB_W, s_len), _BF16),
        scratch_shapes=[
            pltpu.VMEM((V_EXT_ROWS, s_len), _BF16),
            *[pltpu.VMEM((3 * BLOCK, GROUP * BLOCK), _F32) for _ in range(B_BUFS)],
            *[pltpu.VMEM((1, GROUP * BLOCK), _F32) for _ in range(B_BUFS)],
        ],
        compiler_params=pltpu.CompilerParams(
            dimension_semantics=("parallel", "parallel"), vmem_limit_bytes=VMEM_LIMIT_V7X),
        name="attn_b",
    )(qt, k, vt, bias_t, sink_row)


def _post_kernel(x_ref, oa_ref, ob_ref, wo_ref, wup_ref, wdn_ref, g1_ref, g2_ref, g3_ref, out_ref):
    ts = out_ref.shape[1]
    slabs = [slice(lo, lo + POST_SLAB) for lo in range(0, ts, POST_SLAB)]
    x1s, hs = [], []
    for rows in slabs:
        mix = _dot_tn(oa_ref[0, :, rows], wo_ref[0:QA_W, :]) + _dot_tn(ob_ref[0, :, rows], wo_ref[QA_W:, :])
        x1 = x_ref[0, rows, :] + _rms_rows(mix, g1_ref[...])
        x1s.append(x1)
        hs.append(_rms_rows(x1, g2_ref[...]).astype(_BF16))
    for rows, x1, h in zip(slabs, x1s, hs):
        f = None
        for c in range(D_FF // FF_CHUNK):
            u = jnp.maximum(_dot(h, wup_ref[:, c * FF_CHUNK:(c + 1) * FF_CHUNK]), 0.0)
            fc = _dot((u * u).astype(_BF16), wdn_ref[c * FF_CHUNK:(c + 1) * FF_CHUNK, :])
            f = fc if f is None else f + fc
        out_ref[0, rows, :] = x1 + _rms_rows(f, g3_ref[...])


def _post(x, oa, ob, wo, wup, wdn, g1, g2, g3):
    bsz, s_len, _ = x.shape
    ts = POST_TS
    const = lambda b, i: (0, 0)
    resident = functools.partial(pl.BlockSpec, index_map=const, pipeline_mode=pl.Buffered(1))
    return pl.pallas_call(
        _post_kernel,
        grid=(bsz, s_len // ts),
        in_specs=[
            pl.BlockSpec((1, ts, D_MODEL), lambda b, i: (b, i, 0)),
            pl.BlockSpec((1, QA_W, ts), lambda b, i: (b, 0, i)),
            pl.BlockSpec((1, QB_W, ts), lambda b, i: (b, 0, i)),
            resident((D_MODEL, D_MODEL)),
            resident((D_MODEL, D_FF)),
            resident((D_FF, D_MODEL)),
            pl.BlockSpec((1, D_MODEL), const),
            pl.BlockSpec((1, D_MODEL), const),
            pl.BlockSpec((1, D_MODEL), const),
        ],
        out_specs=pl.BlockSpec((1, ts, D_MODEL), lambda b, i: (b, i, 0)),
        out_shape=jax.ShapeDtypeStruct((bsz, s_len, D_MODEL), _F32),
        compiler_params=pltpu.CompilerParams(
            dimension_semantics=("parallel", "parallel"), vmem_limit_bytes=VMEM_LIMIT_V7X),
        name="post",
    )(x, oa, ob, wo, wup, wdn, g1, g2, g3)


def _rope_tables(s_len):
    nf = HEAD_DIM // 4
    pos = jnp.arange(s_len, dtype=jnp.int32)
    row = (pos // GRID_W).astype(_F32)
    col = (pos % GRID_W).astype(_F32)
    freqs = ROPE_THETA ** (-jnp.arange(nf, dtype=_F32) / nf)
    ang_r = row[:, None] * freqs[None, :]
    ang_c = col[:, None] * freqs[None, :]
    cr, sr, cc, sc = jnp.cos(ang_r), jnp.sin(ang_r), jnp.cos(ang_c), jnp.sin(ang_c)
    cos_t = jnp.concatenate([cr, cr, cc, cc], axis=1).T
    sin_t = jnp.concatenate([-sr, sr, -sc, sc], axis=1).T
    return cos_t, sin_t


def _t5_bucket_t():
    nb = N_BUCKETS // 2
    a = jnp.arange(BLOCK, dtype=jnp.int32)
    c = jnp.arange(3 * BLOCK, dtype=jnp.int32)
    rel = c[:, None] - BLOCK - a[None, :]
    ret = (rel > 0).astype(jnp.int32) * nb
    n = jnp.abs(rel)
    max_exact = nb // 2
    nf = jnp.maximum(n, 1).astype(_F32)
    large = max_exact + (jnp.log(nf / max_exact) / np.float32(np.log(MAX_DISTANCE / max_exact))
                         * (nb - max_exact)).astype(jnp.int32)
    large = jnp.minimum(large, nb - 1)
    return ret + jnp.where(n < max_exact, n, large)


def _partner_gain(g):
    g4 = g.reshape(4, HEAD_DIM // 4)
    return jnp.stack([g4[1], g4[0], g4[3], g4[2]]).reshape(HEAD_DIM, 1)


def kernel(x, w_in, w_o, g_pre_mix, g_post_mix, q_norm_a, k_norm_a, sink_b, rel_bias,
           g_pre_ffn, w_ffn_up, w_ffn_down, g_post_ffn):
    bsz, s_len, _ = x.shape
    assert s_len % max(PROJ_TS, ATT_TQ, ATT_TK, POST_TS) == 0 and B_UNROLL % B_BUFS == 0
    cos_t, sin_t = _rope_tables(s_len)
    bucket_t = _t5_bucket_t()
    bias_t = _bias(rel_bias * LOG2E, bucket_t)

    for l in range(w_in.shape[0]):
        gq, gk = q_norm_a[l], k_norm_a[l]
        sink_row = jnp.repeat(sink_b[l] * LOG2E, BLOCK).reshape(N_KV_B, 1, GROUP * BLOCK)
        qa, ka, va, qb, kb, vb = _proj(
            x, g_pre_mix[l][None], w_in[l].T.astype(_BF16), cos_t, sin_t,
            gq.reshape(HEAD_DIM, 1), _partner_gain(gq), gk.reshape(HEAD_DIM, 1), _partner_gain(gk))
        oa = _attn_a(qa, ka, va)
        ob = _attn_b(qb, kb, vb, bias_t, sink_row)
        x = _post(x, oa, ob, w_o[l].astype(_BF16), w_ffn_up[l].astype(_BF16), w_ffn_down[l].astype(_BF16),
                  g_post_mix[l][None], g_pre_ffn[l][None], g_post_ffn[l][None])
    return x
```

```python
import functools
import math

import numpy as np
import jax
import jax.numpy as jnp
from jax import lax
from jax.experimental import pallas as pl
from jax.experimental.pallas import tpu as pltpu

D_MODEL = 1024
HEAD_DIM = 64
N_HEADS_A = 8
N_KV_A = 2
N_HEADS_B = 8
N_KV_B = 2
GROUP = 4
D_FF = 4 * D_MODEL
GRID_W = 64
BLOCK = 128
WINDOW = 128
N_BUCKETS = 32
MAX_DISTANCE = 128
ROPE_THETA = 10000.0
EPS = 1e-6
NEG_INF = -1e30
QA_W = N_HEADS_A * HEAD_DIM
KV_W = N_KV_A * HEAD_DIM
QB_W = N_HEADS_B * HEAD_DIM
IN_TOTAL = QA_W + 2 * KV_W + QB_W + 2 * KV_W
LOG2E = math.log2(math.e)
Q_SCALE = HEAD_DIM ** -0.5 * LOG2E

VMEM_LIMIT_V7X = 56 * 1024 * 1024

PROJ_TS = 2048
PROJ_SLAB = 1024
PROJ_PIECE = 256
ATT_TQ = 512
ATT_TK = 256
A_UNROLL = 4
B_LAG = 1
V_EXT_ROWS = HEAD_DIM + 16
B_BUFS = 4
B_UNROLL = 12
POST_TS = 1024
POST_SLAB = 256
FF_CHUNK = 1024

_F32 = jnp.float32
_BF16 = jnp.bfloat16


def _dot(a, b):
    return jnp.dot(a, b, preferred_element_type=_F32)


def _dot_nt(a, b):
    return lax.dot_general(a, b, (((1,), (1,)), ((), ())), preferred_element_type=_F32)


def _dot_tn(a, b):
    return lax.dot_general(a, b, (((0,), (0,)), ((), ())), preferred_element_type=_F32)


def _rms_rows(x, g):
    ms = jnp.mean(x * x, axis=-1, keepdims=True)
    return x * lax.rsqrt(ms + EPS) * g


def _rope_partner(t):
    return jnp.concatenate([t[16:32], t[0:16], t[48:64], t[32:48]], axis=0)


def _proj_kernel(x_ref, g_ref, wt_ref, cos_ref, sin_ref, gq_ref, gqp_ref, gk_ref, gkp_ref,
                 qa_ref, ka_ref, va_ref, qb_ref, kb_ref, vb_ref):
    def norm_rope(blk, coef_a, coef_b):
        ms = jnp.mean(blk * blk, axis=0, keepdims=True)
        t = blk * lax.rsqrt(ms + EPS)
        return t * coef_a + _rope_partner(t) * coef_b

    ts = x_ref.shape[1]
    for t0 in range(0, ts, PROJ_SLAB):
        tok = slice(t0, t0 + PROJ_SLAB)
        h = _rms_rows(x_ref[0, tok, :], g_ref[...]).astype(_BF16)

        def piece(lo, rows):
            return _dot_nt(wt_ref[lo:lo + rows, :], h)

        cos = cos_ref[:, tok]
        sin = sin_ref[:, tok]
        qa_a = gq_ref[...] * cos * Q_SCALE
        qa_b = gqp_ref[...] * sin * Q_SCALE
        for lo in range(0, QA_W, PROJ_PIECE):
            pt = piece(lo, PROJ_PIECE)
            for j in range(PROJ_PIECE // HEAD_DIM):
                blk = pt[j * HEAD_DIM:(j + 1) * HEAD_DIM]
                r = lo + j * HEAD_DIM
                qa_ref[0, r:r + HEAD_DIM, tok] = norm_rope(blk, qa_a, qa_b).astype(_BF16)

        off = QA_W
        pt = piece(off, 2 * KV_W)
        ka_a = gk_ref[...] * cos
        ka_b = gkp_ref[...] * sin
        kat = jnp.concatenate(
            [norm_rope(pt[j * HEAD_DIM:(j + 1) * HEAD_DIM], ka_a, ka_b) for j in range(N_KV_A)],
            axis=0)
        ka_ref[0, tok, :] = kat.T.astype(_BF16)
        va_ref[0, :, tok] = pt[KV_W:].astype(_BF16)

        off += 2 * KV_W
        for lo in range(0, QB_W, PROJ_PIECE):
            qb_ref[0, lo:lo + PROJ_PIECE, tok] = (piece(off + lo, PROJ_PIECE) * Q_SCALE).astype(_BF16)

        off += QB_W
        pt = piece(off, 2 * KV_W)
        kb_ref[0, tok, :] = pt[:KV_W].T.astype(_BF16)
        vb_ref[0, :, tok] = pt[KV_W:].astype(_BF16)


def _proj(x, g_pre, wt, cos_t, sin_t, gq, gqp, gk, gkp):
    bsz, s_len, _ = x.shape
    ts = PROJ_TS
    const = lambda b, i: (0, 0)
    fm = lambda b, i: (b, 0, i)
    tm = lambda b, i: (b, i, 0)
    col = pl.BlockSpec((HEAD_DIM, 1), const)
    out_shape = (
        jax.ShapeDtypeStruct((bsz, QA_W, s_len), _BF16),
        jax.ShapeDtypeStruct((bsz, s_len, KV_W), _BF16),
        jax.ShapeDtypeStruct((bsz, KV_W, s_len), _BF16),
        jax.ShapeDtypeStruct((bsz, QB_W, s_len), _BF16),
        jax.ShapeDtypeStruct((bsz, s_len, KV_W), _BF16),
        jax.ShapeDtypeStruct((bsz, KV_W, s_len), _BF16),
    )
    return pl.pallas_call(
        _proj_kernel,
        grid=(bsz, s_len // ts),
        in_specs=[
            pl.BlockSpec((1, ts, D_MODEL), tm),
            pl.BlockSpec((1, D_MODEL), const),
            pl.BlockSpec((IN_TOTAL, D_MODEL), const),
            pl.BlockSpec((HEAD_DIM, ts), lambda b, i: (0, i)),
            pl.BlockSpec((HEAD_DIM, ts), lambda b, i: (0, i)),
            col, col, col, col,
        ],
        out_specs=(
            pl.BlockSpec((1, QA_W, ts), fm),
            pl.BlockSpec((1, ts, KV_W), tm),
            pl.BlockSpec((1, KV_W, ts), fm),
            pl.BlockSpec((1, QB_W, ts), fm),
            pl.BlockSpec((1, ts, KV_W), tm),
            pl.BlockSpec((1, KV_W, ts), fm),
        ),
        out_shape=out_shape,
        compiler_params=pltpu.CompilerParams(
            dimension_semantics=("parallel", "parallel"), vmem_limit_bytes=VMEM_LIMIT_V7X),
        name="proj",
    )(x, g_pre, wt, cos_t, sin_t, gq, gqp, gk, gkp)


def _bias_kernel(table_ref, bucket_ref, out_ref):
    bucket = bucket_ref[...]
    c_idx = lax.broadcasted_iota(jnp.int32, bucket.shape, 0)
    a_idx = lax.broadcasted_iota(jnp.int32, bucket.shape, 1)
    band = jnp.abs(c_idx - BLOCK - a_idx) <= WINDOW
    edge_valid = (None, c_idx >= BLOCK, c_idx < 2 * BLOCK)
    for h in range(N_HEADS_B):
        acc = jnp.zeros(bucket.shape, _F32)
        for k in range(N_BUCKETS):
            acc = jnp.where(bucket == k, table_ref[k, h], acc)
        g, j = divmod(h, GROUP)
        for e, ok in enumerate(edge_valid):
            valid = band if ok is None else band & ok
            out_ref[g, e, :, j * BLOCK:(j + 1) * BLOCK] = jnp.where(valid, acc, NEG_INF)


def _bias(table, bucket_t):
    span = 3 * BLOCK
    shape = (N_KV_B, 3, span, GROUP * BLOCK)
    return pl.pallas_call(
        _bias_kernel,
        in_specs=[
            pl.BlockSpec(memory_space=pltpu.SMEM),
            pl.BlockSpec((span, BLOCK), lambda: (0, 0)),
        ],
        out_specs=pl.BlockSpec(shape, lambda: (0, 0, 0, 0)),
        out_shape=jax.ShapeDtypeStruct(shape, _F32),
        name="bias",
    )(table, bucket_t)


def _pad_q(q, g):
    zero = jnp.zeros_like(q)
    return jnp.concatenate([jnp.where(g == 0, q, zero), jnp.where(g == 1, q, zero)], axis=0)


def _attn_a_kernel(q_ref, k_ref, v_ref, o_ref, v_scr, s0, s1, m0, m1, *, tq, tk):
    s_len = k_ref.shape[1]
    nq = s_len // tq
    n_tiles = N_HEADS_A * nq

    ones = jnp.ones((V_EXT_ROWS - HEAD_DIM, s_len), _BF16)
    for kv in range(N_KV_A):
        v_scr[kv, 0:HEAD_DIM, :] = v_ref[0, kv * HEAD_DIM:(kv + 1) * HEAD_DIM, :]
        v_scr[kv, HEAD_DIM:, :] = ones

    def tile_slices(t):
        j = t // nq
        i = t - j * nq
        return (j // GROUP, pl.ds(pl.multiple_of(j * HEAD_DIM, HEAD_DIM), HEAD_DIM),
                pl.ds(pl.multiple_of(i * tq, tq), tq))

    n_chunks = s_len // tk

    def step(t_a, a_bufs, t_b, b_bufs):
        if t_a is not None:
            g_a, rows_a, cols_a = tile_slices(t_a)
            qp = _pad_q(q_ref[0, rows_a, cols_a], g_a)
            mx = None
        if t_b is not None:
            g_b, rows_b, cols_b = tile_slices(t_b)
            m_b = b_bufs[1][...]
            acc = jnp.zeros((V_EXT_ROWS, tq), _F32)
        for c in range(n_chunks + B_LAG):
            if t_a is not None and c < n_chunks:
                keys = slice(c * tk, (c + 1) * tk)
                s = _dot(k_ref[0, keys, :], qp)
                a_bufs[0][keys, :] = s
                cm = jnp.max(s.reshape(tk // 8, 8, tq), axis=0)
                mx = cm if mx is None else jnp.maximum(mx, cm)
            if t_b is not None and c >= B_LAG:
                keys = slice((c - B_LAG) * tk, (c - B_LAG + 1) * tk)
                p = jnp.exp2(b_bufs[0][keys, :] - m_b).astype(_BF16)
                acc = acc + _dot(v_scr[g_b, :, keys], p)
        if t_a is not None:
            a_bufs[1][...] = jnp.max(mx, axis=0, keepdims=True)
        if t_b is not None:
            o_ref[0, rows_b, cols_b] = (acc[:HEAD_DIM] / acc[HEAD_DIM:HEAD_DIM + 1]).astype(_BF16)

    buf0, buf1 = (s0, m0), (s1, m1)
    step(0, buf0, None, None)

    bufs = (buf0, buf1)

    def steps(t0, count):
        for i in range(count):
            step(t0 + i + 1, bufs[(i + 1) % 2], t0 + i, bufs[i % 2])

    def body(u, carry):
        steps(u * A_UNROLL, A_UNROLL)
        return carry

    n_loop = (n_tiles - 1) // A_UNROLL
    lax.fori_loop(0, n_loop, body, 0)
    done = n_loop * A_UNROLL
    steps(done, n_tiles - 1 - done)
    step(None, None, n_tiles - 1, bufs[(n_tiles - 1) % 2])


def _attn_a(qt, k, vt):
    bsz, _, s_len = qt.shape
    tq = ATT_TQ
    kern = functools.partial(_attn_a_kernel, tq=tq, tk=ATT_TK)
    whole = lambda b: (b, 0, 0)
    return pl.pallas_call(
        kern,
        grid=(bsz,),
        in_specs=[
            pl.BlockSpec((1, QA_W, s_len), whole),
            pl.BlockSpec((1, s_len, KV_W), whole),
            pl.BlockSpec((1, KV_W, s_len), whole),
        ],
        out_specs=pl.BlockSpec((1, QA_W, s_len), whole),
        out_shape=jax.ShapeDtypeStruct((bsz, QA_W, s_len), _BF16),
        scratch_shapes=[
            pltpu.VMEM((N_KV_A, V_EXT_ROWS, s_len), _BF16),
            pltpu.VMEM((s_len, tq), _F32),
            pltpu.VMEM((s_len, tq), _F32),
            pltpu.VMEM((1, tq), _F32),
            pltpu.VMEM((1, tq), _F32),
        ],
        compiler_params=pltpu.CompilerParams(
            dimension_semantics=("parallel",), vmem_limit_bytes=VMEM_LIMIT_V7X),
        name="attn_a",
    )(qt, k, vt)


def _attn_b_kernel(q_ref, k_ref, v_ref, bias_ref, sink_ref, o_ref, v_scr, *sm, n_blocks):
    g = pl.program_id(0)
    s_len = n_blocks * BLOCK
    v_scr[0:HEAD_DIM, :] = v_ref[0]
    v_scr[HEAD_DIM:, :] = jnp.ones((V_EXT_ROWS - HEAD_DIM, s_len), _BF16)
    sink = sink_ref[0]

    def span_starts(n):
        return [pl.multiple_of(jnp.clip(n + d, 0, n_blocks - 1) * BLOCK, BLOCK) for d in (-1, 0, 1)]

    def stage_a(n, s_buf, m_buf):
        cols = pl.ds(pl.multiple_of(n * BLOCK, BLOCK), BLOCK)
        q4 = jnp.concatenate(
            [q_ref[0, j * HEAD_DIM:(j + 1) * HEAD_DIM, cols] for j in range(GROUP)], axis=1)
        k_span = jnp.concatenate([k_ref[0, pl.ds(st, BLOCK), :] for st in span_starts(n)], axis=0)
        edge = jnp.where(n == 0, 1, jnp.where(n == n_blocks - 1, 2, 0))
        s = _dot(k_span, _pad_q(q4, g)) + bias_ref[0, edge]
        s_buf[...] = s
        m_buf[...] = jnp.maximum(jnp.max(s, axis=0, keepdims=True), sink)

    def stage_b(n, s_buf, m_buf):
        cols = pl.ds(pl.multiple_of(n * BLOCK, BLOCK), BLOCK)
        m = m_buf[...]
        p = jnp.exp2(s_buf[...] - m).astype(_BF16)
        v_span = jnp.concatenate([v_scr[:, pl.ds(st, BLOCK)] for st in span_starts(n)], axis=1)
        pv = _dot(v_span, p)
        denom = pv[HEAD_DIM:HEAD_DIM + 1] + jnp.exp2(sink - m)
        o = (pv[:HEAD_DIM] / denom).astype(_BF16)
        for j in range(GROUP):
            o_ref[0, j * HEAD_DIM:(j + 1) * HEAD_DIM, cols] = o[:, j * BLOCK:(j + 1) * BLOCK]

    bufs = tuple(zip(sm[:B_BUFS], sm[B_BUFS:]))
    ahead = B_BUFS - 1

    def steps(n0, count):
        for i in range(count):
            stage_a(n0 + i + ahead, *bufs[(i + ahead) % B_BUFS])
            stage_b(n0 + i, *bufs[i % B_BUFS])

    for n in range(ahead):
        stage_a(n, *bufs[n])
    n_loop = (n_blocks - ahead) // B_UNROLL

    def body(u, carry):
        steps(u * B_UNROLL, B_UNROLL)
        return carry

    lax.fori_loop(0, n_loop, body, 0)
    done = n_loop * B_UNROLL
    steps(done, n_blocks - ahead - done)
    for n in range(n_blocks - ahead, n_blocks):
        stage_b(n, *bufs[n % B_BUFS])


def _attn_b(qt, k, vt, bias_t, sink_row):
    bsz, _, s_len = qt.shape
    n_blocks = s_len // BLOCK
    gw = GROUP * HEAD_DIM
    kern = functools.partial(_attn_b_kernel, n_blocks=n_blocks)
    return pl.pallas_call(
        kern,
        grid=(N_KV_B, bsz),
        in_specs=[
            pl.BlockSpec((1, gw, s_len), lambda g, b: (b, g, 0)),
            pl.BlockSpec((1, s_len, KV_W), lambda g, b: (b, 0, 0)),
            pl.BlockSpec((1, HEAD_DIM, s_len), lambda g, b: (b, g, 0)),
            pl.BlockSpec((1, 3, 3 * BLOCK, GROUP * BLOCK), lambda g, b: (g, 0, 0, 0)),
            pl.BlockSpec((1, 1, GROUP * BLOCK), lambda g, b: (g, 0, 0)),
        ],
        out_specs=pl.BlockSpec((1, gw, s_len), lambda g, b: (b, g, 0)),
        out_shape=jax.ShapeDtypeStruct((bsz, QB_W, s_len), _BF16),
        scratch_shapes=[
            pltpu.VMEM((V_EXT_ROWS, s_len), _BF16),
            *[pltpu.VMEM((3 * BLOCK, GROUP * BLOCK), _F32) for _ in range(B_BUFS)],
            *[pltpu.VMEM((1, GROUP * BLOCK), _F32) for _ in range(B_BUFS)],
        ],
        compiler_params=pltpu.CompilerParams(
            dimension_semantics=("parallel", "parallel"), vmem_limit_bytes=VMEM_LIMIT_V7X),
        name="attn_b",
    )(qt, k, vt, bias_t, sink_row)


def _post_kernel(x_ref, oa_ref, ob_ref, wo_ref, wup_ref, wdn_ref, g1_ref, g2_ref, g3_ref, out_ref):
    ts = out_ref.shape[1]
    slabs = [slice(lo, lo + POST_SLAB) for lo in range(0, ts, POST_SLAB)]
    x1s, hs = [], []
    for rows in slabs:
        mix = _dot_tn(oa_ref[0, :, rows], wo_ref[0:QA_W, :]) + _dot_tn(ob_ref[0, :, rows], wo_ref[QA_W:, :])
        x1 = x_ref[0, rows, :] + _rms_rows(mix, g1_ref[...])
        x1s.append(x1)
        hs.append(_rms_rows(x1, g2_ref[...]).astype(_BF16))
    for rows, x1, h in zip(slabs, x1s, hs):
        f = None
        for c in range(D_FF // FF_CHUNK):
            u = jnp.maximum(_dot(h, wup_ref[:, c * FF_CHUNK:(c + 1) * FF_CHUNK]), 0.0)
            fc = _dot((u * u).astype(_BF16), wdn_ref[c * FF_CHUNK:(c + 1) * FF_CHUNK, :])
            f = fc if f is None else f + fc
        out_ref[0, rows, :] = x1 + _rms_rows(f, g3_ref[...])


def _post(x, oa, ob, wo, wup, wdn, g1, g2, g3):
    bsz, s_len, _ = x.shape
    ts = POST_TS
    const = lambda b, i: (0, 0)
    resident = functools.partial(pl.BlockSpec, index_map=const, pipeline_mode=pl.Buffered(1))
    return pl.pallas_call(
        _post_kernel,
        grid=(bsz, s_len // ts),
        in_specs=[
            pl.BlockSpec((1, ts, D_MODEL), lambda b, i: (b, i, 0)),
            pl.BlockSpec((1, QA_W, ts), lambda b, i: (b, 0, i)),
            pl.BlockSpec((1, QB_W, ts), lambda b, i: (b, 0, i)),
            resident((D_MODEL, D_MODEL)),
            resident((D_MODEL, D_FF)),
            resident((D_FF, D_MODEL)),
            pl.BlockSpec((1, D_MODEL), const),
            pl.BlockSpec((1, D_MODEL), const),
            pl.BlockSpec((1, D_MODEL), const),
        ],
        out_specs=pl.BlockSpec((1, ts, D_MODEL), lambda b, i: (b, i, 0)),
        out_shape=jax.ShapeDtypeStruct((bsz, s_len, D_MODEL), _F32),
        compiler_params=pltpu.CompilerParams(
            dimension_semantics=("parallel", "parallel"), vmem_limit_bytes=VMEM_LIMIT_V7X),
        name="post",
    )(x, oa, ob, wo, wup, wdn, g1, g2, g3)


def _rope_tables(s_len):
    nf = HEAD_DIM // 4
    pos = jnp.arange(s_len, dtype=jnp.int32)
    row = (pos // GRID_W).astype(_F32)
    col = (pos % GRID_W).astype(_F32)
    freqs = ROPE_THETA ** (-jnp.arange(nf, dtype=_F32) / nf)
    ang_r = row[:, None] * freqs[None, :]
    ang_c = col[:, None] * freqs[None, :]
    cr, sr, cc, sc = jnp.cos(ang_r), jnp.sin(ang_r), jnp.cos(ang_c), jnp.sin(ang_c)
    cos_t = jnp.concatenate([cr, cr, cc, cc], axis=1).T
    sin_t = jnp.concatenate([-sr, sr, -sc, sc], axis=1).T
    return cos_t, sin_t


def _t5_bucket_t():
    nb = N_BUCKETS // 2
    a = jnp.arange(BLOCK, dtype=jnp.int32)
    c = jnp.arange(3 * BLOCK, dtype=jnp.int32)
    rel = c[:, None] - BLOCK - a[None, :]
    ret = (rel > 0).astype(jnp.int32) * nb
    n = jnp.abs(rel)
    max_exact = nb // 2
    nf = jnp.maximum(n, 1).astype(_F32)
    large = max_exact + (jnp.log(nf / max_exact) / np.float32(np.log(MAX_DISTANCE / max_exact))
                         * (nb - max_exact)).astype(jnp.int32)
    large = jnp.minimum(large, nb - 1)
    return ret + jnp.where(n < max_exact, n, large)


def _partner_gain(g):
    g4 = g.reshape(4, HEAD_DIM // 4)
    return jnp.stack([g4[1], g4[0], g4[3], g4[2]]).reshape(HEAD_DIM, 1)


def kernel(x, w_in, w_o, g_pre_mix, g_post_mix, q_norm_a, k_norm_a, sink_b, rel_bias,
           g_pre_ffn, w_ffn_up, w_ffn_down, g_post_ffn):
    bsz, s_len, _ = x.shape
    assert s_len % max(PROJ_TS, ATT_TQ, ATT_TK, POST_TS) == 0 and B_UNROLL % B_BUFS == 0
    cos_t, sin_t = _rope_tables(s_len)
    bucket_t = _t5_bucket_t()
    bias_t = _bias(rel_bias * LOG2E, bucket_t)

    for l in range(w_in.shape[0]):
        gq, gk = q_norm_a[l], k_norm_a[l]
        sink_row = jnp.repeat(sink_b[l] * LOG2E, BLOCK).reshape(N_KV_B, 1, GROUP * BLOCK)
        qa, ka, va, qb, kb, vb = _proj(
            x, g_pre_mix[l][None], w_in[l].T.astype(_BF16), cos_t, sin_t,
            gq.reshape(HEAD_DIM, 1), _partner_gain(gq), gk.reshape(HEAD_DIM, 1), _partner_gain(gk))
        oa = _attn_a(qa, ka, va)
        ob = _attn_b(qb, kb, vb, bias_t, sink_row)
        x = _post(x, oa, ob, w_o[l].astype(_BF16), w_ffn_up[l].astype(_BF16), w_ffn_down[l].astype(_BF16),
                  g_post_mix[l][None], g_pre_ffn[l][None], g_post_ffn[l][None])
    return x
```

```python
import functools
import math

import numpy as np
import jax
import jax.numpy as jnp
from jax import lax
from jax.experimental import pallas as pl
from jax.experimental.pallas import tpu as pltpu

D_MODEL = 1024
HEAD_DIM = 64
N_HEADS_A = 8
N_KV_A = 2
N_HEADS_B = 8
N_KV_B = 2
GROUP = 4
D_FF = 4 * D_MODEL
GRID_W = 64
BLOCK = 128
WINDOW = 128
N_BUCKETS = 32
MAX_DISTANCE = 128
ROPE_THETA = 10000.0
EPS = 1e-6
NEG_INF = -1e30
QA_W = N_HEADS_A * HEAD_DIM
KV_W = N_KV_A * HEAD_DIM
QB_W = N_HEADS_B * HEAD_DIM
IN_TOTAL = QA_W + 2 * KV_W + QB_W + 2 * KV_W
LOG2E = math.log2(math.e)
Q_SCALE = HEAD_DIM ** -0.5 * LOG2E

VMEM_LIMIT_V7X = 56 * 1024 * 1024

PROJ_TS = 2048
PROJ_SLAB = 1024
PROJ_PIECE = 256
ATT_TQ = 512
ATT_TK = 256
A_UNROLL = 4
B_LAG = 1
V_EXT_ROWS = HEAD_DIM + 16
B_BUFS = 4
B_UNROLL = 12
POST_TS = 1024
POST_SLAB = 256
FF_CHUNK = 1024

_F32 = jnp.float32
_BF16 = jnp.bfloat16


def _dot(a, b):
    return jnp.dot(a, b, preferred_element_type=_F32)


def _dot_nt(a, b):
    return lax.dot_general(a, b, (((1,), (1,)), ((), ())), preferred_element_type=_F32)


def _dot_tn(a, b):
    return lax.dot_general(a, b, (((0,), (0,)), ((), ())), preferred_element_type=_F32)


def _rms_rows(x, g):
    ms = jnp.mean(x * x, axis=-1, keepdims=True)
    return x * lax.rsqrt(ms + EPS) * g


def _rope_partner(t):
    return jnp.concatenate([t[16:32], t[0:16], t[48:64], t[32:48]], axis=0)


def _proj_kernel(x_ref, g_ref, wt_ref, cos_ref, sin_ref, gq_ref, gqp_ref, gk_ref, gkp_ref,
                 qa_ref, ka_ref, va_ref, qb_ref, kb_ref, vb_ref):
    def norm_rope(blk, coef_a, coef_b):
        ms = jnp.mean(blk * blk, axis=0, keepdims=True)
        t = blk * lax.rsqrt(ms + EPS)
        return t * coef_a + _rope_partner(t) * coef_b

    ts = x_ref.shape[1]
    for t0 in range(0, ts, PROJ_SLAB):
        tok = slice(t0, t0 + PROJ_SLAB)
        h = _rms_rows(x_ref[0, tok, :], g_ref[...]).astype(_BF16)

        def piece(lo, rows):
            return _dot_nt(wt_ref[lo:lo + rows, :], h)

        cos = cos_ref[:, tok]
        sin = sin_ref[:, tok]
        qa_a = gq_ref[...] * cos * Q_SCALE
        qa_b = gqp_ref[...] * sin * Q_SCALE
        for lo in range(0, QA_W, PROJ_PIECE):
            pt = piece(lo, PROJ_PIECE)
            for j in range(PROJ_PIECE // HEAD_DIM):
                blk = pt[j * HEAD_DIM:(j + 1) * HEAD_DIM]
                r = lo + j * HEAD_DIM
                qa_ref[0, r:r + HEAD_DIM, tok] = norm_rope(blk, qa_a, qa_b).astype(_BF16)

        off = QA_W
        pt = piece(off, 2 * KV_W)
        ka_a = gk_ref[...] * cos
        ka_b = gkp_ref[...] * sin
        kat = jnp.concatenate(
            [norm_rope(pt[j * HEAD_DIM:(j + 1) * HEAD_DIM], ka_a, ka_b) for j in range(N_KV_A)],
            axis=0)
        ka_ref[0, tok, :] = kat.T.astype(_BF16)
        va_ref[0, :, tok] = pt[KV_W:].astype(_BF16)

        off += 2 * KV_W
        for lo in range(0, QB_W, PROJ_PIECE):
            qb_ref[0, lo:lo + PROJ_PIECE, tok] = (piece(off + lo, PROJ_PIECE) * Q_SCALE).astype(_BF16)

        off += QB_W
        pt = piece(off, 2 * KV_W)
        kb_ref[0, tok, :] = pt[:KV_W].T.astype(_BF16)
        vb_ref[0, :, tok] = pt[KV_W:].astype(_BF16)


def _proj(x, g_pre, wt, cos_t, sin_t, gq, gqp, gk, gkp):
    bsz, s_len, _ = x.shape
    ts = PROJ_TS
    const = lambda b, i: (0, 0)
    fm = lambda b, i: (b, 0, i)
    tm = lambda b, i: (b, i, 0)
    col = pl.BlockSpec((HEAD_DIM, 1), const)
    out_shape = (
        jax.ShapeDtypeStruct((bsz, QA_W, s_len), _BF16),
        jax.ShapeDtypeStruct((bsz, s_len, KV_W), _BF16),
        jax.ShapeDtypeStruct((bsz, KV_W, s_len), _BF16),
        jax.ShapeDtypeStruct((bsz, QB_W, s_len), _BF16),
        jax.ShapeDtypeStruct((bsz, s_len, KV_W), _BF16),
        jax.ShapeDtypeStruct((bsz, KV_W, s_len), _BF16),
    )
    return pl.pallas_call(
        _proj_kernel,
        grid=(bsz, s_len // ts),
        in_specs=[
            pl.BlockSpec((1, ts, D_MODEL), tm),
            pl.BlockSpec((1, D_MODEL), const),
            pl.BlockSpec((IN_TOTAL, D_MODEL), const),
            pl.BlockSpec((HEAD_DIM, ts), lambda b, i: (0, i)),
            pl.BlockSpec((HEAD_DIM, ts), lambda b, i: (0, i)),
            col, col, col, col,
        ],
        out_specs=(
            pl.BlockSpec((1, QA_W, ts), fm),
            pl.BlockSpec((1, ts, KV_W), tm),
            pl.BlockSpec((1, KV_W, ts), fm),
            pl.BlockSpec((1, QB_W, ts), fm),
            pl.BlockSpec((1, ts, KV_W), tm),
            pl.BlockSpec((1, KV_W, ts), fm),
        ),
        out_shape=out_shape,
        compiler_params=pltpu.CompilerParams(
            dimension_semantics=("parallel", "parallel"), vmem_limit_bytes=VMEM_LIMIT_V7X),
        name="proj",
    )(x, g_pre, wt, cos_t, sin_t, gq, gqp, gk, gkp)


def _bias_kernel(table_ref, bucket_ref, out_ref):
    bucket = bucket_ref[...]
    c_idx = lax.broadcasted_iota(jnp.int32, bucket.shape, 0)
    a_idx = lax.broadcasted_iota(jnp.int32, bucket.shape, 1)
    band = jnp.abs(c_idx - BLOCK - a_idx) <= WINDOW
    edge_valid = (None, c_idx >= BLOCK, c_idx < 2 * BLOCK)
    for h in range(N_HEADS_B):
        acc = jnp.zeros(bucket.shape, _F32)
        for k in range(N_BUCKETS):
            acc = jnp.where(bucket == k, table_ref[k, h], acc)
        g, j = divmod(h, GROUP)
        for e, ok in enumerate(edge_valid):
            valid = band if ok is None else band & ok
            out_ref[g, e, :, j * BLOCK:(j + 1) * BLOCK] = jnp.where(valid, acc, NEG_INF)


def _bias(table, bucket_t):
    span = 3 * BLOCK
    shape = (N_KV_B, 3, span, GROUP * BLOCK)
    return pl.pallas_call(
        _bias_kernel,
        in_specs=[
            pl.BlockSpec(memory_space=pltpu.SMEM),
            pl.BlockSpec((span, BLOCK), lambda: (0, 0)),
        ],
        out_specs=pl.BlockSpec(shape, lambda: (0, 0, 0, 0)),
        out_shape=jax.ShapeDtypeStruct(shape, _F32),
        name="bias",
    )(table, bucket_t)


def _pad_q(q, g):
    zero = jnp.zeros_like(q)
    return jnp.concatenate([jnp.where(g == 0, q, zero), jnp.where(g == 1, q, zero)], axis=0)


def _attn_a_kernel(q_ref, k_ref, v_ref, o_ref, v_scr, s0, s1, m0, m1, *, tq, tk):
    s_len = k_ref.shape[1]
    nq = s_len // tq
    n_tiles = N_HEADS_A * nq

    ones = jnp.ones((V_EXT_ROWS - HEAD_DIM, s_len), _BF16)
    for kv in range(N_KV_A):
        v_scr[kv, 0:HEAD_DIM, :] = v_ref[0, kv * HEAD_DIM:(kv + 1) * HEAD_DIM, :]
        v_scr[kv, HEAD_DIM:, :] = ones

    def tile_slices(t):
        j = t // nq
        i = t - j * nq
        return (j // GROUP, pl.ds(pl.multiple_of(j * HEAD_DIM, HEAD_DIM), HEAD_DIM),
                pl.ds(pl.multiple_of(i * tq, tq), tq))

    n_chunks = s_len // tk

    def step(t_a, a_bufs, t_b, b_bufs):
        if t_a is not None:
            g_a, rows_a, cols_a = tile_slices(t_a)
            qp = _pad_q(q_ref[0, rows_a, cols_a], g_a)
            mx = None
        if t_b is not None:
            g_b, rows_b, cols_b = tile_slices(t_b)
            m_b = b_bufs[1][...]
            acc = jnp.zeros((V_EXT_ROWS, tq), _F32)
        for c in range(n_chunks + B_LAG):
            if t_a is not None and c < n_chunks:
                keys = slice(c * tk, (c + 1) * tk)
                s = _dot(k_ref[0, keys, :], qp)
                a_bufs[0][keys, :] = s
                cm = jnp.max(s.reshape(tk // 8, 8, tq), axis=0)
                mx = cm if mx is None else jnp.maximum(mx, cm)
            if t_b is not None and c >= B_LAG:
                keys = slice((c - B_LAG) * tk, (c - B_LAG + 1) * tk)
                p = jnp.exp2(b_bufs[0][keys, :] - m_b).astype(_BF16)
                acc = acc + _dot(v_scr[g_b, :, keys], p)
        if t_a is not None:
            a_bufs[1][...] = jnp.max(mx, axis=0, keepdims=True)
        if t_b is not None:
            o_ref[0, rows_b, cols_b] = (acc[:HEAD_DIM] / acc[HEAD_DIM:HEAD_DIM + 1]).astype(_BF16)

    buf0, buf1 = (s0, m0), (s1, m1)
    step(0, buf0, None, None)

    bufs = (buf0, buf1)

    def steps(t0, count):
        for i in range(count):
            step(t0 + i + 1, bufs[(i + 1) % 2], t0 + i, bufs[i % 2])

    def body(u, carry):
        steps(u * A_UNROLL, A_UNROLL)
        return carry

    n_loop = (n_tiles - 1) // A_UNROLL
    lax.fori_loop(0, n_loop, body, 0)
    done = n_loop * A_UNROLL
    steps(done, n_tiles - 1 - done)
    step(None, None, n_tiles - 1, bufs[(n_tiles - 1) % 2])


def _attn_a(qt, k, vt):
    bsz, _, s_len = qt.shape
    tq = ATT_TQ
    kern = functools.partial(_attn_a_kernel, tq=tq, tk=ATT_TK)
    whole = lambda b: (b, 0, 0)
    return pl.pallas_call(
        kern,
        grid=(bsz,),
        in_specs=[
            pl.BlockSpec((1, QA_W, s_len), whole),
            pl.BlockSpec((1, s_len, KV_W), whole),
            pl.BlockSpec((1, KV_W, s_len), whole),
        ],
        out_specs=pl.BlockSpec((1, QA_W, s_len), whole),
        out_shape=jax.ShapeDtypeStruct((bsz, QA_W, s_len), _BF16),
        scratch_shapes=[
            pltpu.VMEM((N_KV_A, V_EXT_ROWS, s_len), _BF16),
            pltpu.VMEM((s_len, tq), _F32),
            pltpu.VMEM((s_len, tq), _F32),
            pltpu.VMEM((1, tq), _F32),
            pltpu.VMEM((1, tq), _F32),
        ],
        compiler_params=pltpu.CompilerParams(
            dimension_semantics=("parallel",), vmem_limit_bytes=VMEM_LIMIT_V7X),
        name="attn_a",
    )(qt, k, vt)


def _attn_b_kernel(table_ref, bucket_ref, q_ref, k_ref, v_ref, sink_ref, o_ref, bias_ref, v_scr, *sm, n_blocks):
    g = pl.program_id(0)

    @pl.when(pl.program_id(1) == 0)
    def _():
        bucket = bucket_ref[...]
        c_idx = lax.broadcasted_iota(jnp.int32, bucket.shape, 0)
        a_idx = lax.broadcasted_iota(jnp.int32, bucket.shape, 1)
        band = jnp.abs(c_idx - BLOCK - a_idx) <= WINDOW
        edge_valid = (None, c_idx >= BLOCK, c_idx < 2 * BLOCK)
        for j in range(GROUP):
            acc = jnp.zeros(bucket.shape, _F32)
            for k in range(N_BUCKETS):
                acc = jnp.where(bucket == k, table_ref[k, g * GROUP + j], acc)
            for e, ok in enumerate(edge_valid):
                valid = band if ok is None else band & ok
                bias_ref[e, :, j * BLOCK:(j + 1) * BLOCK] = jnp.where(valid, acc, NEG_INF)

    s_len = n_blocks * BLOCK
    v_scr[0:HEAD_DIM, :] = v_ref[0]
    v_scr[HEAD_DIM:, :] = jnp.ones((V_EXT_ROWS - HEAD_DIM, s_len), _BF16)
    sink = sink_ref[0]

    def span_starts(n):
        return [pl.multiple_of(jnp.clip(n + d, 0, n_blocks - 1) * BLOCK, BLOCK) for d in (-1, 0, 1)]

    def stage_a(n, s_buf, m_buf):
        cols = pl.ds(pl.multiple_of(n * BLOCK, BLOCK), BLOCK)
        q4 = jnp.concatenate(
            [q_ref[0, j * HEAD_DIM:(j + 1) * HEAD_DIM, cols] for j in range(GROUP)], axis=1)
        k_span = jnp.concatenate([k_ref[0, pl.ds(st, BLOCK), :] for st in span_starts(n)], axis=0)
        edge = jnp.where(n == 0, 1, jnp.where(n == n_blocks - 1, 2, 0))
        s = _dot(k_span, _pad_q(q4, g)) + bias_ref[edge]
        s_buf[...] = s
        m_buf[...] = jnp.maximum(jnp.max(s, axis=0, keepdims=True), sink)

    def stage_b(n, s_buf, m_buf):
        cols = pl.ds(pl.multiple_of(n * BLOCK, BLOCK), BLOCK)
        m = m_buf[...]
        p = jnp.exp2(s_buf[...] - m).astype(_BF16)
        v_span = jnp.concatenate([v_scr[:, pl.ds(st, BLOCK)] for st in span_starts(n)], axis=1)
        pv = _dot(v_span, p)
        denom = pv[HEAD_DIM:HEAD_DIM + 1] + jnp.exp2(sink - m)
        o = (pv[:HEAD_DIM] / denom).astype(_BF16)
        for j in range(GROUP):
            o_ref[0, j * HEAD_DIM:(j + 1) * HEAD_DIM, cols] = o[:, j * BLOCK:(j + 1) * BLOCK]

    bufs = tuple(zip(sm[:B_BUFS], sm[B_BUFS:]))
    ahead = B_BUFS - 1

    def steps(n0, count):
        for i in range(count):
            stage_a(n0 + i + ahead, *bufs[(i + ahead) % B_BUFS])
            stage_b(n0 + i, *bufs[i % B_BUFS])

    for n in range(ahead):
        stage_a(n, *bufs[n])
    n_loop = (n_blocks - ahead) // B_UNROLL

    def body(u, carry):
        steps(u * B_UNROLL, B_UNROLL)
        return carry

    lax.fori_loop(0, n_loop, body, 0)
    done = n_loop * B_UNROLL
    steps(done, n_blocks - ahead - done)
    for n in range(n_blocks - ahead, n_blocks):
        stage_b(n, *bufs[n % B_BUFS])


def _attn_b(table, bucket_t, qt, k, vt, sink_row):
    bsz, _, s_len = qt.shape
    n_blocks = s_len // BLOCK
    gw = GROUP * HEAD_DIM
    kern = functools.partial(_attn_b_kernel, n_blocks=n_blocks)
    return pl.pallas_call(
        kern,
        grid=(N_KV_B, bsz),
        in_specs=[
            pl.BlockSpec(memory_space=pltpu.SMEM),
            pl.BlockSpec((3 * BLOCK, BLOCK), lambda g, b: (0, 0)),
            pl.BlockSpec((1, gw, s_len), lambda g, b: (b, g, 0)),
            pl.BlockSpec((1, s_len, KV_W), lambda g, b: (b, 0, 0)),
            pl.BlockSpec((1, HEAD_DIM, s_len), lambda g, b: (b, g, 0)),
            pl.BlockSpec((1, 1, GROUP * BLOCK), lambda g, b: (g, 0, 0)),
        ],
        out_specs=pl.BlockSpec((1, gw, s_len), lambda g, b: (b, g, 0)),
        out_shape=jax.ShapeDtypeStruct((bsz, QB_W, s_len), _BF16),
        scratch_shapes=[
            pltpu.VMEM((3, 3 * BLOCK, GROUP * BLOCK), _F32),
            pltpu.VMEM((V_EXT_ROWS, s_len), _BF16),
            *[pltpu.VMEM((3 * BLOCK, GROUP * BLOCK), _F32) for _ in range(B_BUFS)],
            *[pltpu.VMEM((1, GROUP * BLOCK), _F32) for _ in range(B_BUFS)],
        ],
        compiler_params=pltpu.CompilerParams(
            dimension_semantics=("arbitrary", "arbitrary"), vmem_limit_bytes=VMEM_LIMIT_V7X),
        name="attn_b",
    )(table, bucket_t, qt, k, vt, sink_row)


def _post_kernel(x_ref, oa_ref, ob_ref, wo_ref, wup_ref, wdn_ref, g1_ref, g2_ref, g3_ref, out_ref):
    ts = out_ref.shape[1]
    slabs = [slice(lo, lo + POST_SLAB) for lo in range(0, ts, POST_SLAB)]
    x1s, hs = [], []
    for rows in slabs:
        mix = _dot_tn(oa_ref[0, :, rows], wo_ref[0:QA_W, :]) + _dot_tn(ob_ref[0, :, rows], wo_ref[QA_W:, :])
        x1 = x_ref[0, rows, :] + _rms_rows(mix, g1_ref[...])
        x1s.append(x1)
        hs.append(_rms_rows(x1, g2_ref[...]).astype(_BF16))
    for rows, x1, h in zip(slabs, x1s, hs):
        f = None
        for c in range(D_FF // FF_CHUNK):
            u = jnp.maximum(_dot(h, wup_ref[:, c * FF_CHUNK:(c + 1) * FF_CHUNK]), 0.0)
            fc = _dot((u * u).astype(_BF16), wdn_ref[c * FF_CHUNK:(c + 1) * FF_CHUNK, :])
            f = fc if f is None else f + fc
        out_ref[0, rows, :] = x1 + _rms_rows(f, g3_ref[...])


def _post(x, oa, ob, wo, wup, wdn, g1, g2, g3):
    bsz, s_len, _ = x.shape
    ts = POST_TS
    const = lambda b, i: (0, 0)
    resident = functools.partial(pl.BlockSpec, index_map=const, pipeline_mode=pl.Buffered(1))
    return pl.pallas_call(
        _post_kernel,
        grid=(bsz, s_len // ts),
        in_specs=[
            pl.BlockSpec((1, ts, D_MODEL), lambda b, i: (b, i, 0)),
            pl.BlockSpec((1, QA_W, ts), lambda b, i: (b, 0, i)),
            pl.BlockSpec((1, QB_W, ts), lambda b, i: (b, 0, i)),
            resident((D_MODEL, D_MODEL)),
            resident((D_MODEL, D_FF)),
            resident((D_FF, D_MODEL)),
            pl.BlockSpec((1, D_MODEL), const),
            pl.BlockSpec((1, D_MODEL), const),
            pl.BlockSpec((1, D_MODEL), const),
        ],
        out_specs=pl.BlockSpec((1, ts, D_MODEL), lambda b, i: (b, i, 0)),
        out_shape=jax.ShapeDtypeStruct((bsz, s_len, D_MODEL), _F32),
        compiler_params=pltpu.CompilerParams(
            dimension_semantics=("parallel", "parallel"), vmem_limit_bytes=VMEM_LIMIT_V7X),
        name="post",
    )(x, oa, ob, wo, wup, wdn, g1, g2, g3)


def _rope_tables(s_len):
    nf = HEAD_DIM // 4
    pos = jnp.arange(s_len, dtype=jnp.int32)
    row = (pos // GRID_W).astype(_F32)
    col = (pos % GRID_W).astype(_F32)
    freqs = ROPE_THETA ** (-jnp.arange(nf, dtype=_F32) / nf)
    ang_r = row[:, None] * freqs[None, :]
    ang_c = col[:, None] * freqs[None, :]
    cr, sr, cc, sc = jnp.cos(ang_r), jnp.sin(ang_r), jnp.cos(ang_c), jnp.sin(ang_c)
    cos_t = jnp.concatenate([cr, cr, cc, cc], axis=1).T
    sin_t = jnp.concatenate([-sr, sr, -sc, sc], axis=1).T
    return cos_t, sin_t


def _t5_bucket_t():
    nb = N_BUCKETS // 2
    a = jnp.arange(BLOCK, dtype=jnp.int32)
    c = jnp.arange(3 * BLOCK, dtype=jnp.int32)
    rel = c[:, None] - BLOCK - a[None, :]
    ret = (rel > 0).astype(jnp.int32) * nb
    n = jnp.abs(rel)
    max_exact = nb // 2
    nf = jnp.maximum(n, 1).astype(_F32)
    large = max_exact + (jnp.log(nf / max_exact) / np.float32(np.log(MAX_DISTANCE / max_exact))
                         * (nb - max_exact)).astype(jnp.int32)
    large = jnp.minimum(large, nb - 1)
    return ret + jnp.where(n < max_exact, n, large)


def _partner_gain(g):
    g4 = g.reshape(4, HEAD_DIM // 4)
    return jnp.stack([g4[1], g4[0], g4[3], g4[2]]).reshape(HEAD_DIM, 1)


def kernel(x, w_in, w_o, g_pre_mix, g_post_mix, q_norm_a, k_norm_a, sink_b, rel_bias,
           g_pre_ffn, w_ffn_up, w_ffn_down, g_post_ffn):
    bsz, s_len, _ = x.shape
    assert s_len % max(PROJ_TS, ATT_TQ, ATT_TK, POST_TS) == 0 and B_UNROLL % B_BUFS == 0
    cos_t, sin_t = _rope_tables(s_len)
    bucket_t = _t5_bucket_t()

    for l in range(w_in.shape[0]):
        gq, gk = q_norm_a[l], k_norm_a[l]
        sink_row = jnp.repeat(sink_b[l] * LOG2E, BLOCK).reshape(N_KV_B, 1, GROUP * BLOCK)
        qa, ka, va, qb, kb, vb = _proj(
            x, g_pre_mix[l][None], w_in[l].T.astype(_BF16), cos_t, sin_t,
            gq.reshape(HEAD_DIM, 1), _partner_gain(gq), gk.reshape(HEAD_DIM, 1), _partner_gain(gk))
        oa = _attn_a(qa, ka, va)
        ob = _attn_b(rel_bias * LOG2E, bucket_t, qb, kb, vb, sink_row)
        x = _post(x, oa, ob, w_o[l].astype(_BF16), w_ffn_up[l].astype(_BF16), w_ffn_down[l].astype(_BF16),
                  g_post_mix[l][None], g_pre_ffn[l][None], g_post_ffn[l][None])
    return x
```
